```python
import math, functools
import jax, jax.numpy as jnp
from jax import lax
import numpy as np

D_MODEL = 1024
BATCH = 4
SEQ = 4096
DEPTH = 1
DEC_BATCH = 128
DEC_SEQ = 8
PAST_LEN = 8192
PAGE_SIZE = 128

H_A = 8
DK_A = 128
DV_A = 128
CONV_W = 4
DN_CHUNK = 64
H_B = 8
HD_B = 128
Q_BLOCK = 128
FORGET_BIAS = 3.0
N_EXPERTS = 32
TOP_K = 4
D_FF = D_MODEL
SWIGLU_LIMIT = 7.0
SWIGLU_ALPHA = 1.702
PLE_DIM = 256
EPS = 1e-6

QK_A = H_A * DK_A
V_A = H_A * DV_A
CONV_CH = 2 * QK_A + V_A
W_B = H_B * HD_B
IN_SPLIT_SIZES = (QK_A, QK_A, V_A, V_A, H_A, H_A, W_B, W_B, W_B, H_B, D_MODEL, D_MODEL)
IN_COLS = 2 * QK_A + 2 * V_A + 2 * H_A + 3 * W_B + H_B + 2 * D_MODEL

kernel_name = 'gdn_fox_parallel_moe_ple_step'


def rms_norm(x, g):
    xf = x.astype(jnp.float32)
    y = xf * lax.rsqrt(jnp.mean(xf * xf, axis=-1, keepdims=True) + EPS)
    return (y * g.astype(jnp.float32)).astype(x.dtype)


def l2_normalize(x):
    xf = x.astype(jnp.float32)
    return xf * lax.rsqrt(jnp.sum(xf * xf, axis=-1, keepdims=True) + EPS)


def split_columns(proj):
    outs, start = [], 0
    for size in IN_SPLIT_SIZES:
        outs.append(proj[..., start:start + size])
        start += size
    return outs


def short_conv(u, buf, w):
    L = u.shape[1]
    full = jnp.concatenate([buf.astype(u.dtype), u], axis=1)
    out = full[:, 0:L] * w[0]
    for i in range(1, CONV_W):
        out = out + full[:, i:i + L] * w[i]
    return jax.nn.silu(out), full[:, L:]


def gated_delta_rule(q, k, v, g, beta, s0, chunk):
    B, L, H, DK = q.shape
    DV = v.shape[-1]
    pad = (-L) % chunk
    if pad:
        padw = lambda t: jnp.pad(t, [(0, 0), (0, pad)] + [(0, 0)] * (t.ndim - 2))
        q, k, v, g, beta = map(padw, (q, k, v, g, beta))
    n = (L + pad) // chunk

    def blocks(t):
        t = t.reshape((B, n, chunk) + t.shape[2:])
        return jnp.moveaxis(jnp.moveaxis(t, 1, 0), 2, 3)

    q, k, v, g, beta = map(blocks, (q, k, v, g, beta))
    G = jnp.cumsum(g, axis=-1)
    incl = jnp.tril(jnp.ones((chunk, chunk), bool))
    strict = jnp.tril(jnp.ones((chunk, chunk), bool), -1)
    diff = G[..., :, None] - G[..., None, :]
    decay = jnp.where(incl, jnp.exp(jnp.where(incl, diff, 0.0)), 0.0)
    kb = k * beta[..., None]
    m = jnp.where(strict, jnp.einsum('nbhid,nbhjd->nbhij', kb, k) * decay, 0.0)
    a = m + jnp.eye(chunk, dtype=m.dtype)
    rhs = jnp.concatenate([v * beta[..., None], kb * jnp.exp(G)[..., None]], axis=-1)
    sol = lax.linalg.triangular_solve(a, rhs, left_side=True, lower=True, unit_diagonal=True)
    u_base, w_dec = sol[..., :DV], sol[..., DV:]
    qk = jnp.einsum('nbhid,nbhjd->nbhij', q, k) * decay
    q_dec = q * jnp.exp(G)[..., None]
    k_dec = k * jnp.exp(G[..., -1:] - G)[..., None]
    g_last = jnp.exp(G[..., -1])[..., None, None]

    def step(s, xs):
        u_b, w_d, qk_c, q_d, k_d, gl = xs
        u = u_b - jnp.einsum('bhcd,bhde->bhce', w_d, s)
        o = jnp.einsum('bhcd,bhde->bhce', q_d, s) + jnp.einsum('bhij,bhje->bhie', qk_c, u)
        s = s * gl + jnp.einsum('bhcd,bhce->bhde', k_d, u)
        return s, o

    s_fin, o = lax.scan(step, s0, (u_base, w_dec, qk, q_dec, k_dec, g_last))
    o = jnp.moveaxis(jnp.moveaxis(o, 3, 2), 0, 1).reshape(B, n * chunk, H, DV)[:, :L]
    return o, s_fin


def fox_prompt(q, k, v, logf):
    B, S, H, D = q.shape
    nb = S // Q_BLOCK
    c = jnp.cumsum(logf.astype(jnp.float32), axis=1).transpose(0, 2, 1)
    qb = jnp.moveaxis(q.reshape(B, nb, Q_BLOCK, H, D), 1, 0)
    cb = jnp.moveaxis(c.reshape(B, H, nb, Q_BLOCK), 2, 0)
    kpos = jnp.arange(S)
    scale = D ** -0.5

    def block(args):
        qi, ci, start = args
        s = jnp.einsum('bqhd,bkhd->bhqk', qi, k).astype(jnp.float32) * scale
        s = s + ci[..., None] - c[:, :, None, :]
        qpos = start + jnp.arange(Q_BLOCK)
        s = jnp.where(kpos[None, :] <= qpos[:, None], s, -jnp.inf)
        p = jax.nn.softmax(s, axis=-1)
        return jnp.einsum('bhqk,bkhd->bqhd', p.astype(v.dtype), v)

    o = lax.map(block, (qb, cb, jnp.arange(nb) * Q_BLOCK))
    return jnp.moveaxis(o, 0, 1).reshape(B, S, H, D)


def fox_sample(q, k, v, logf, cache_k, cache_v, cache_logf, page_table, layer):
    L = q.shape[1]
    scale = HD_B ** -0.5
    causal = jnp.tril(jnp.ones((L, L), bool))

    def one(args):
        qs, ks, vs, lfs, pages = args
        k_past = cache_k[layer, pages].reshape(-1, H_B, HD_B)
        v_past = cache_v[layer, pages].reshape(-1, H_B, HD_B)
        c_past = jnp.cumsum(cache_logf[layer, pages].reshape(-1, H_B).astype(jnp.float32), axis=0).T
        c_new = c_past[:, -1:] + jnp.cumsum(lfs.astype(jnp.float32), axis=0).T
        s_past = jnp.einsum('qhd,khd->hqk', qs, k_past).astype(jnp.float32) * scale
        s_past = s_past + c_new[:, :, None] - c_past[:, None, :]
        s_new = jnp.einsum('qhd,khd->hqk', qs, ks).astype(jnp.float32) * scale
        s_new = jnp.where(causal, s_new + c_new[:, :, None] - c_new[:, None, :], -jnp.inf)
        p = jax.nn.softmax(jnp.concatenate([s_past, s_new], axis=-1), axis=-1).astype(vs.dtype)
        P = k_past.shape[0]
        return (jnp.einsum('hqk,khd->qhd', p[..., :P], v_past)
                + jnp.einsum('hqk,khd->qhd', p[..., P:], vs))

    return lax.map(one, (q, k, v, logf, page_table))


def moe_ffn(h, w_router, b_router, w_gate, b_gate, w_up, b_up, w_down, b_down):
    logits = (h @ w_router + b_router).astype(jnp.float32)
    top_val, top_idx = lax.top_k(logits, TOP_K)
    top_w = jax.nn.softmax(top_val, axis=-1)
    gate_w = jnp.sum(jax.nn.one_hot(top_idx, N_EXPERTS, dtype=jnp.float32) * top_w[..., None],
                     axis=-2).astype(h.dtype)
    out = jnp.zeros_like(h)
    for e in range(N_EXPERTS):
        gt = jnp.minimum(h @ w_gate[e] + b_gate[e], SWIGLU_LIMIT)
        up = jnp.clip(h @ w_up[e] + b_up[e], -SWIGLU_LIMIT, SWIGLU_LIMIT)
        act = (up + 1.0) * gt * jax.nn.sigmoid(SWIGLU_ALPHA * gt)
        out = out + gate_w[..., e:e + 1] * (act @ w_down[e] + b_down[e])
    return out


def hybrid_layer(x, pe, lw, conv_buf, s0, attend):
    B, L, _ = x.shape
    h = rms_norm(x, lw['norm_mix'])
    qa, ka, va, za, ba, aa, qb, kb, vb, fb, ga, gb = split_columns(h @ lw['w_in'])
    u, new_conv = short_conv(jnp.concatenate([qa, ka, va], axis=-1), conv_buf, lw['conv_w'])
    qa = l2_normalize(u[..., :QK_A].reshape(B, L, H_A, DK_A)) * (DK_A ** -0.5)
    ka = l2_normalize(u[..., QK_A:2 * QK_A].reshape(B, L, H_A, DK_A))
    va = u[..., 2 * QK_A:].reshape(B, L, H_A, DV_A).astype(jnp.float32)
    g = -jnp.exp(lw['a_log'].astype(jnp.float32)) * jax.nn.softplus(
        aa.astype(jnp.float32) + lw['dt_bias'].astype(jnp.float32))
    beta = jax.nn.sigmoid(ba.astype(jnp.float32))
    oa, s_new = gated_delta_rule(qa, ka, va, g, beta, s0.astype(jnp.float32), min(DN_CHUNK, L))
    oa = rms_norm(oa.astype(x.dtype), lw['norm_o_a']) * jax.nn.silu(za.reshape(B, L, H_A, DV_A))
    ya = oa.reshape(B, L, V_A) @ lw['w_o_a']
    logf = jax.nn.log_sigmoid(fb.astype(jnp.float32) + lw['b_f'].astype(jnp.float32))
    qb = qb.reshape(B, L, H_B, HD_B)
    kb = kb.reshape(B, L, H_B, HD_B)
    vb = vb.reshape(B, L, H_B, HD_B)
    ob = attend(qb, kb, vb, logf)
    yb = ob.reshape(B, L, W_B) @ lw['w_o_b']
    mixed = jax.nn.sigmoid(ga) * ya + jax.nn.sigmoid(gb) * yb
    x = x + mixed @ lw['w_out']
    x = x + moe_ffn(rms_norm(x, lw['norm_ffn']), lw['w_router'], lw['b_router'], lw['w_gate'],
                    lw['b_gate'], lw['w_up'], lw['b_up'], lw['w_down'], lw['b_down'])
    x = x + (pe @ lw['w_ple_proj']) * jax.nn.sigmoid(rms_norm(x, lw['norm_ple']) @ lw['w_ple_gate'])
    return x, (kb, vb, logf, s_new, new_conv)


def setup_inputs(seed: int = 0) -> dict:
    key = jax.random.key(seed)
    keys = jax.random.split(key, 40)
    f32 = jnp.float32
    n_pages = PAST_LEN // PAGE_SIZE
    n_pool = (5 * DEC_BATCH * n_pages) // 4

    def nrm(i, shape, scale=1.0):
        return scale * jax.random.normal(keys[i], shape, f32)

    def gain(i, n):
        return 1.0 + nrm(i, (DEPTH, n), 0.02)

    x_prompt = nrm(0, (BATCH, SEQ, D_MODEL))
    x_sample = nrm(1, (DEC_BATCH, DEC_SEQ, D_MODEL))
    p_prompt = nrm(2, (DEPTH, BATCH, SEQ, PLE_DIM))
    p_sample = nrm(3, (DEPTH, DEC_BATCH, DEC_SEQ, PLE_DIM))
    cache_k = nrm(4, (DEPTH, n_pool, PAGE_SIZE, H_B, HD_B))
    cache_v = nrm(5, (DEPTH, n_pool, PAGE_SIZE, H_B, HD_B))
    cache_logf = jax.nn.log_sigmoid(FORGET_BIAS + nrm(6, (DEPTH, n_pool, PAGE_SIZE, H_B)))
    state_S = nrm(7, (DEPTH, DEC_BATCH, H_A, DK_A, DV_A), 0.1)
    state_conv = nrm(8, (DEPTH, DEC_BATCH, CONV_W - 1, CONV_CH))
    page_table = jax.random.permutation(keys[9], n_pool)[:DEC_BATCH * n_pages].reshape(
        DEC_BATCH, n_pages).astype(jnp.int32)
    norm_mix = gain(10, D_MODEL)
    w_in = nrm(11, (DEPTH, D_MODEL, IN_COLS), D_MODEL ** -0.5)
    conv_w = nrm(12, (DEPTH, CONV_W, CONV_CH), CONV_W ** -0.5)
    a_log = jnp.log(jax.random.uniform(keys[13], (DEPTH, H_A), f32, 1.0, 16.0))
    dt = jnp.exp(jax.random.uniform(keys[14], (DEPTH, H_A), f32, math.log(1e-3), math.log(1e-1)))
    dt_bias = dt + jnp.log(-jnp.expm1(-dt))
    norm_o_a = gain(15, DV_A)
    w_o_a = nrm(16, (DEPTH, V_A, D_MODEL), V_A ** -0.5)
    b_f = FORGET_BIAS + nrm(17, (DEPTH, H_B), 0.1)
    w_o_b = nrm(18, (DEPTH, W_B, D_MODEL), W_B ** -0.5)
    w_out = nrm(19, (DEPTH, D_MODEL, D_MODEL), D_MODEL ** -0.5)
    norm_ffn = gain(20, D_MODEL)
    w_router = nrm(21, (DEPTH, D_MODEL, N_EXPERTS), D_MODEL ** -0.5)
    b_router = nrm(22, (DEPTH, N_EXPERTS), 0.01)
    w_gate = nrm(23, (DEPTH, N_EXPERTS, D_MODEL, D_FF), D_MODEL ** -0.5)
    b_gate = nrm(24, (DEPTH, N_EXPERTS, D_FF), 0.01)
    w_up = nrm(25, (DEPTH, N_EXPERTS, D_MODEL, D_FF), D_MODEL ** -0.5)
    b_up = nrm(26, (DEPTH, N_EXPERTS, D_FF), 0.01)
    w_down = nrm(27, (DEPTH, N_EXPERTS, D_FF, D_MODEL), D_FF ** -0.5)
    b_down = nrm(28, (DEPTH, N_EXPERTS, D_MODEL), 0.01)
    norm_ple = gain(29, D_MODEL)
    w_ple_gate = nrm(30, (DEPTH, D_MODEL, D_MODEL), D_MODEL ** -0.5)
    w_ple_proj = nrm(31, (DEPTH, PLE_DIM, D_MODEL), PLE_DIM ** -0.5)
    norm_final = 1.0 + nrm(32, (D_MODEL,), 0.02)
    return {'x_prompt': x_prompt, 'x_sample': x_sample, 'p_prompt': p_prompt, 'p_sample': p_sample,
            'cache_k': cache_k, 'cache_v': cache_v, 'cache_logf': cache_logf,
            'state_S': state_S, 'state_conv': state_conv, 'page_table': page_table,
            'norm_mix': norm_mix, 'w_in': w_in, 'conv_w': conv_w, 'a_log': a_log, 'dt_bias': dt_bias,
            'norm_o_a': norm_o_a, 'w_o_a': w_o_a, 'b_f': b_f, 'w_o_b': w_o_b, 'w_out': w_out,
            'norm_ffn': norm_ffn, 'w_router': w_router, 'b_router': b_router,
            'w_gate': w_gate, 'b_gate': b_gate, 'w_up': w_up, 'b_up': b_up,
            'w_down': w_down, 'b_down': b_down, 'norm_ple': norm_ple,
            'w_ple_gate': w_ple_gate, 'w_ple_proj': w_ple_proj, 'norm_final': norm_final}


def reference(x_prompt, x_sample, p_prompt, p_sample, cache_k, cache_v, cache_logf, state_S,
              state_conv, page_table, norm_mix, w_in, conv_w, a_log, dt_bias, norm_o_a, w_o_a,
              b_f, w_o_b, w_out, norm_ffn, w_router, b_router, w_gate, b_gate, w_up, b_up,
              w_down, b_down, norm_ple, w_ple_gate, w_ple_proj, norm_final):
    xp, xs = x_prompt, x_sample
    new_p, new_s = [], []
    for i in range(DEPTH):
        lw = {'norm_mix': norm_mix[i], 'w_in': w_in[i], 'conv_w': conv_w[i], 'a_log': a_log[i],
              'dt_bias': dt_bias[i], 'norm_o_a': norm_o_a[i], 'w_o_a': w_o_a[i], 'b_f': b_f[i],
              'w_o_b': w_o_b[i], 'w_out': w_out[i], 'norm_ffn': norm_ffn[i],
              'w_router': w_router[i], 'b_router': b_router[i], 'w_gate': w_gate[i],
              'b_gate': b_gate[i], 'w_up': w_up[i], 'b_up': b_up[i], 'w_down': w_down[i],
              'b_down': b_down[i], 'norm_ple': norm_ple[i], 'w_ple_gate': w_ple_gate[i],
              'w_ple_proj': w_ple_proj[i]}
        bp = xp.shape[0]
        conv0 = jnp.zeros((bp, CONV_W - 1, CONV_CH), xp.dtype)
        s0 = jnp.zeros((bp, H_A, DK_A, DV_A), jnp.float32)
        xp, st_p = hybrid_layer(xp, p_prompt[i], lw, conv0, s0, fox_prompt)
        attend_s = functools.partial(fox_sample, cache_k=cache_k, cache_v=cache_v,
                                     cache_logf=cache_logf, page_table=page_table, layer=i)
        xs, st_s = hybrid_layer(xs, p_sample[i], lw, state_conv[i], state_S[i], attend_s)
        new_p.append(st_p)
        new_s.append(st_s)

    def stack(states, j):
        return jnp.stack([st[j] for st in states])

    y_prompt = rms_norm(xp, norm_final)
    y_sample = rms_norm(xs, norm_final)
    return (y_prompt, y_sample,
            stack(new_p, 0), stack(new_p, 1), stack(new_p, 2), stack(new_p, 3), stack(new_p, 4),
            stack(new_s, 0), stack(new_s, 1), stack(new_s, 2), stack(new_s, 3), stack(new_s, 4))
```

```python
import functools

import jax
import jax.numpy as jnp
from jax import lax
from jax.experimental import pallas as pl
from jax.experimental.pallas import tpu as pltpu

F32 = jnp.float32
BF16 = jnp.bfloat16
I32 = jnp.int32

D_MODEL = 1024
N_HEADS = 8
HEAD_DIM = 128
CONV_W = 4
CONV_CH = 3 * D_MODEL
DN_CHUNK = 64
N_EXPERTS = 32
TOP_K = 4
PAGE_SIZE = 128
SWIGLU_LIMIT = 7.0
SWIGLU_ALPHA = 1.702
EPS = 1e-6
NEG_BIG = -1e30

G_QA, G_KA, G_VA, G_ZA, G_QB, G_KB, G_VB, G_GA, G_GB = range(9)
N_GROUPS = 9
SMALL_ROWS = 32

MIB = 1024 * 1024

ROUTER_LANES = 128

ROW_TILE = 512
MERGE_TILE = 256
MOE_TILE = 512
GATHER_TILE = 256
PAGES_PER_STEP = 8


def _params(sem, vmem_mib):
    return pltpu.CompilerParams(dimension_semantics=sem, vmem_limit_bytes=vmem_mib * MIB)


def _row_tile(n, cap):
    t = cap
    while n % t:
        t //= 2
    assert t >= 8, (n, cap)
    return t


def _bdot(a, b):
    return jnp.dot(a.astype(BF16), b.astype(BF16), preferred_element_type=F32)


def _bdot_nt(a, b):
    return lax.dot_general(a.astype(BF16), b.astype(BF16), (((1,), (1,)), ((), ())),
                           preferred_element_type=F32)


def _bdot_tn(a, b):
    return lax.dot_general(a.astype(BF16), b.astype(BF16), (((0,), (0,)), ((), ())),
                           preferred_element_type=F32)


def _split3(x):
    hi = x.astype(BF16)
    r1 = x - hi.astype(F32)
    mid = r1.astype(BF16)
    lo = (r1 - mid.astype(F32)).astype(BF16)
    return hi, mid, lo


def _sigmoid(x):
    return jax.nn.sigmoid(x)


def _silu(x):
    return x * jax.nn.sigmoid(x)


def _softplus(x):
    return jnp.maximum(x, 0.0) + jnp.log1p(jnp.exp(-jnp.abs(x)))


def _log_sigmoid(x):
    return jnp.minimum(x, 0.0) - jnp.log1p(jnp.exp(-jnp.abs(x)))


def _rms(x, g):
    return x * lax.rsqrt(jnp.mean(x * x, axis=-1, keepdims=True) + EPS) * g


def _in_proj_kernel(x_ref, g_ref, w_ref, ws_ref, big_ref, st_ref, h_ref):
    @pl.when(pl.program_id(1) == 0)
    def _():
        hb = _rms(x_ref[...], g_ref[...]).astype(BF16)
        h_ref[...] = hb
        st_ref[...] = lax.dot_general(ws_ref[...], hb, (((1,), (1,)), ((), ())),
                                      preferred_element_type=F32)

    big_ref[...] = jnp.dot(h_ref[...], w_ref[...], preferred_element_type=F32)


def in_proj(x, gain, w_big, w_small_t):
    n = x.shape[0]
    tm = _row_tile(n, ROW_TILE)
    return pl.pallas_call(
        _in_proj_kernel,
        grid=(n // tm, N_GROUPS),
        in_specs=[
            pl.BlockSpec((tm, D_MODEL), lambda i, g: (i, 0)),
            pl.BlockSpec((1, D_MODEL), lambda i, g: (0, 0)),
            pl.BlockSpec((D_MODEL, D_MODEL), lambda i, g: (0, g)),
            pl.BlockSpec((SMALL_ROWS, D_MODEL), lambda i, g: (0, 0)),
        ],
        out_specs=[
            pl.BlockSpec((None, tm, D_MODEL), lambda i, g: (g, i, 0)),
            pl.BlockSpec((SMALL_ROWS, tm), lambda i, g: (0, i)),
        ],
        out_shape=[
            jax.ShapeDtypeStruct((N_GROUPS, n, D_MODEL), F32),
            jax.ShapeDtypeStruct((SMALL_ROWS, n), F32),
        ],
        scratch_shapes=[pltpu.VMEM((tm, D_MODEL), BF16)],
        compiler_params=_params(("arbitrary", "arbitrary"),32),
        name="in_proj",
    )(x, gain, w_big, w_small_t)


def _logf_kernel(st_ref, bf_ref, lf_ref, c_ref, *, seg):
    lf = _log_sigmoid(st_ref[16:24, :] + bf_ref[...])
    lf_ref[...] = lf
    pos = lax.broadcasted_iota(I32, lf.shape, 1) % seg
    y = lf
    k = 1
    while k < seg:
        y = y + jnp.where(pos >= k, pltpu.roll(y, k, axis=1), 0.0)
        k *= 2
    c_ref[...] = y


def logf_scan(small_t, b_f, seg, block):
    n = small_t.shape[1]
    return pl.pallas_call(
        functools.partial(_logf_kernel, seg=seg),
        grid=(n // block,),
        in_specs=[
            pl.BlockSpec((SMALL_ROWS, block), lambda i: (0, i)),
            pl.BlockSpec((N_HEADS, 1), lambda i: (0, 0)),
        ],
        out_specs=[
            pl.BlockSpec((N_HEADS, block), lambda i: (0, i)),
            pl.BlockSpec((N_HEADS, block), lambda i: (0, i)),
        ],
        out_shape=[jax.ShapeDtypeStruct((N_HEADS, n), F32)] * 2,
        compiler_params=_params(("arbitrary",),16),
        name="logf_scan",
    )(small_t, b_f)


def _unit_lower_inverse(m, c, ii, jj, eye):
    def same_block(s):
        return (ii // s) == (jj // s)

    d = jnp.where(same_block(8), m, 0.0)
    d2 = _bdot(d, d)
    d4 = _bdot(d2, d2)
    x = eye - d
    x = x + _bdot(x, d2)
    x = x + _bdot(x, d4)
    s = 8
    while s < c:
        off = jnp.where(same_block(2 * s) & jnp.logical_not(same_block(s)), m, 0.0)
        x = x - _bdot(_bdot(x, off), x)
        s *= 2
    return x


def _gdn_kernel(q_ref, k_ref, v_ref, z_ref, sm_ref, cb_ref, s0_ref, cw_ref, al_ref, dt_ref, no_ref,
                o_ref, s_ref, full_ref, *, c):
    n = pl.program_id(1)

    @pl.when(n == 0)
    def _():
        full_ref[0:8, :] = cb_ref[...]
        s_ref[...] = s0_ref[...]

    full_ref[8:8 + c, 0:D_MODEL] = q_ref[...]
    full_ref[8:8 + c, D_MODEL:2 * D_MODEL] = k_ref[...]
    full_ref[8:8 + c, 2 * D_MODEL:3 * D_MODEL] = v_ref[...]

    conv = cw_ref[0:1, :] * full_ref[5:5 + c, :]
    for i in range(1, CONV_W):
        conv = conv + cw_ref[i:i + 1, :] * full_ref[5 + i:5 + i + c, :]
    u = _silu(conv)
    tail = full_ref[c:c + 8, :]
    full_ref[0:8, :] = tail

    sm = sm_ref[...]
    beta_all = _sigmoid(sm[0:8, :])
    g_all = -jnp.exp(al_ref[...]) * _softplus(sm[8:16, :] + dt_ref[...])

    ii = lax.broadcasted_iota(I32, (c, c), 0)
    jj = lax.broadcasted_iota(I32, (c, c), 1)
    incl = ii >= jj
    strict = ii > jj
    eye = (ii == jj).astype(F32)
    lower_b = incl.astype(BF16)
    upper_b = (ii <= jj).astype(BF16)
    eye_b = eye.astype(BF16)

    g_row = jnp.zeros((N_HEADS, c), F32)
    g_col = jnp.zeros((c, N_HEADS), F32)
    beta_col = jnp.zeros((c, N_HEADS), F32)
    for gp, bp in zip(_split3(g_all), _split3(beta_all)):
        g_row = g_row + jnp.dot(gp, upper_b, preferred_element_type=F32)
        g_col = g_col + lax.dot_general(lower_b, gp, (((1,), (1,)), ((), ())),
                                        preferred_element_type=F32)
        beta_col = beta_col + lax.dot_general(eye_b, bp, (((1,), (1,)), ((), ())),
                                              preferred_element_type=F32)

    for h in range(N_HEADS):
        sl = slice(h * HEAD_DIM, (h + 1) * HEAD_DIM)
        qh = u[:, sl]
        kh = u[:, D_MODEL + h * HEAD_DIM:D_MODEL + (h + 1) * HEAD_DIM]
        vh = u[:, 2 * D_MODEL + h * HEAD_DIM:2 * D_MODEL + (h + 1) * HEAD_DIM]
        qh = qh * lax.rsqrt(jnp.sum(qh * qh, axis=-1, keepdims=True) + EPS) * (HEAD_DIM ** -0.5)
        kh = kh * lax.rsqrt(jnp.sum(kh * kh, axis=-1, keepdims=True) + EPS)

        gc = g_col[:, h:h + 1]
        gr = g_row[h:h + 1, :]
        bc = beta_col[:, h:h + 1]
        decay = jnp.where(incl, jnp.exp(jnp.where(incl, gc - gr, 0.0)), 0.0)
        kb = kh * bc
        m = jnp.where(strict, _bdot_nt(kb, kh) * decay, 0.0)
        eg = jnp.exp(gc)
        rhs = jnp.concatenate([vh * bc, kb * eg], axis=1)
        sol = _bdot(_unit_lower_inverse(m, c, ii, jj, eye), rhs)
        u_base = sol[:, :HEAD_DIM]
        w_dec = sol[:, HEAD_DIM:]
        qk = _bdot_nt(qh, kh) * decay
        q_dec = qh * eg
        g_last = gr[:, c - 1:c]
        k_dec = kh * jnp.exp(g_last - gc)

        s = s_ref[h]
        uu = u_base - _bdot(w_dec, s)
        o = _bdot(q_dec, s) + _bdot(qk, uu)
        s_ref[h] = s * jnp.exp(g_last) + _bdot_tn(k_dec, uu)

        o_ref[:, sl] = _rms(o, no_ref[...]) * _silu(z_ref[:, sl])


def gdn(big, small_c, conv_buf, s0, conv_w, a_log, dt_bias, norm_o, batch, length, c, tok0):
    n_tok = batch * length
    nc = length // c
    blk0 = tok0 // c

    def grp(g):
        return pl.BlockSpec((None, c, D_MODEL), lambda b, n, g=g: (g, blk0 + b * nc + n, 0))

    def full(shape):
        return pl.BlockSpec(shape, lambda b, n: (0,) * len(shape))

    return pl.pallas_call(
        functools.partial(_gdn_kernel, c=c),
        grid=(batch, nc),
        in_specs=[
            grp(G_QA), grp(G_KA), grp(G_VA), grp(G_ZA),
            pl.BlockSpec((None, SMALL_ROWS, c), lambda b, n: (b * nc + n, 0, 0)),
            pl.BlockSpec((None, 8, CONV_CH), lambda b, n: (b, 0, 0)),
            pl.BlockSpec((None, N_HEADS, HEAD_DIM, HEAD_DIM), lambda b, n: (b, 0, 0, 0)),
            full((CONV_W, CONV_CH)), full((N_HEADS, 1)), full((N_HEADS, 1)), full((1, HEAD_DIM)),
        ],
        out_specs=[
            pl.BlockSpec((c, D_MODEL), lambda b, n: (b * nc + n, 0)),
            pl.BlockSpec((None, N_HEADS, HEAD_DIM, HEAD_DIM), lambda b, n: (b, 0, 0, 0)),
        ],
        out_shape=[
            jax.ShapeDtypeStruct((n_tok, D_MODEL), F32),
            jax.ShapeDtypeStruct((batch, N_HEADS, HEAD_DIM, HEAD_DIM), F32),
        ],
        scratch_shapes=[pltpu.VMEM((c + 8, CONV_CH), F32)],
        compiler_params=_params(("arbitrary", "arbitrary"),32),
        name="gdn",
    )(big, big, big, big, small_c, conv_buf, s0, conv_w, a_log, dt_bias, norm_o)


def _fox_prompt_kernel(q_ref, k_ref, v_ref, c_ref, o_ref, *, tq, tk):
    qi = pl.program_id(2)
    q = (q_ref[...] * (HEAD_DIM ** -0.5)).astype(BF16)
    q0 = pl.multiple_of(qi * tq, tq)
    c_base = jnp.max(c_ref[:, pl.ds(q0, 128)], axis=1, keepdims=True)
    rows = q0 + lax.broadcasted_iota(I32, (tq, tk), 0)
    cols = lax.broadcasted_iota(I32, (tq, tk), 1)

    def body(ki, carry):
        m, l, acc = carry
        k0 = pl.multiple_of(ki * tk, tk)
        kb = k_ref[pl.ds(k0, tk), :].astype(BF16)
        vb = v_ref[pl.ds(k0, tk), :].astype(BF16)
        bias = c_base - c_ref[:, pl.ds(k0, tk)]
        s = lax.dot_general(q, kb, (((1,), (1,)), ((), ())), preferred_element_type=F32) + bias
        s = jnp.where(cols + k0 <= rows, s, NEG_BIG)
        m_new = jnp.maximum(m, jnp.max(s, axis=1, keepdims=True))
        alpha = jnp.exp(m - m_new)
        p = jnp.exp(s - m_new)
        l = l * alpha + jnp.sum(p, axis=1, keepdims=True)
        acc = acc * alpha + jnp.dot(p.astype(BF16), vb, preferred_element_type=F32)
        return m_new, l, acc

    init = (jnp.full((tq, 1), NEG_BIG, F32), jnp.zeros((tq, 1), F32), jnp.zeros((tq, HEAD_DIM), F32))
    n_kv = (q0 + tq + tk - 1) // tk
    _, l, acc = lax.fori_loop(0, n_kv, body, init)
    o_ref[...] = acc / l


def fox_prompt(big, c_t, batch, seq):
    tq = min(512, seq)
    tk = tq
    nq = seq // tq

    def grp(g):
        return pl.BlockSpec((None, seq, HEAD_DIM), lambda b, h, qi, g=g: (g, b, h))

    return pl.pallas_call(
        functools.partial(_fox_prompt_kernel, tq=tq, tk=tk),
        grid=(batch, N_HEADS, nq),
        in_specs=[
            pl.BlockSpec((None, tq, HEAD_DIM), lambda b, h, qi: (G_QB, b * nq + qi, h)),
            grp(G_KB), grp(G_VB),
            pl.BlockSpec((None, None, 1, seq), lambda b, h, qi: (b, h, 0, 0)),
        ],
        out_specs=pl.BlockSpec((tq, HEAD_DIM), lambda b, h, qi: (b * nq + qi, h)),
        out_shape=jax.ShapeDtypeStruct((batch * seq, D_MODEL), F32),
        compiler_params=_params(("arbitrary", "arbitrary", "arbitrary"), 32),
        name="fox_prompt",
    )(big, big, big, c_t)


def _fox_sample_kernel(pt_ref, *refs, n_new, pps):
    del pt_ref
    k_pages = refs[0:pps]
    v_pages = refs[pps:2 * pps]
    f_pages = refs[2 * pps:3 * pps]
    q_ref, kn_ref, vn_ref, cn_ref = refs[3 * pps:3 * pps + 4]
    o_ref = refs[3 * pps + 4]
    qbd_ref, kbuf_ref, vbuf_ref, m_ref, l_ref, acc_ref, carry_ref = refs[3 * pps + 5:]
    step = pl.program_id(1)
    n_rows = N_HEADS * n_new
    span = pps * PAGE_SIZE

    def expand_heads(x):
        return jnp.concatenate([jnp.broadcast_to(x[h:h + 1, :], (n_new, x.shape[1]))
                                for h in range(N_HEADS)], axis=0)

    @pl.when(step == 0)
    def _():
        q = q_ref[...] * (HEAD_DIM ** -0.5)
        q_rep = jnp.concatenate([q] * N_HEADS, axis=0)
        r_head = lax.broadcasted_iota(I32, (n_rows, D_MODEL), 0) // n_new
        c_head = lax.broadcasted_iota(I32, (n_rows, D_MODEL), 1) // HEAD_DIM
        qbd = jnp.where(r_head == c_head, q_rep, 0.0).astype(BF16)
        qbd_ref[...] = qbd
        s = lax.dot_general(qbd, kn_ref[...].astype(BF16), (((1,), (1,)), ((), ())),
                            preferred_element_type=F32)
        s = s - expand_heads(cn_ref[...])
        qpos = lax.broadcasted_iota(I32, (n_rows, n_new), 0) % n_new
        spos = lax.broadcasted_iota(I32, (n_rows, n_new), 1)
        s = jnp.where(spos <= qpos, s, NEG_BIG)
        m = jnp.max(s, axis=1, keepdims=True)
        p = jnp.exp(s - m)
        m_ref[...] = m
        l_ref[...] = jnp.sum(p, axis=1, keepdims=True)
        acc_ref[...] = jnp.dot(p.astype(BF16), vn_ref[...].astype(BF16), preferred_element_type=F32)
        carry_ref[...] = jnp.zeros_like(carry_ref)

    for p_i in range(pps):
        kbuf_ref[p_i * PAGE_SIZE:(p_i + 1) * PAGE_SIZE, :] = k_pages[p_i][...].astype(BF16)
        vbuf_ref[p_i * PAGE_SIZE:(p_i + 1) * PAGE_SIZE, :] = v_pages[p_i][...].astype(BF16)

    lf = jnp.concatenate([f_pages[p_i][...] for p_i in range(pps)], axis=1)
    lane = lax.broadcasted_iota(I32, lf.shape, 1)
    y = lf
    k = 1
    while k < span:
        y = y + jnp.where(lane + k < span, pltpu.roll(y, span - k, axis=1), 0.0)
        k *= 2
    carry = carry_ref[...]
    total = jnp.broadcast_to(y[:, 0:1], carry.shape)
    r = y - lf + jnp.concatenate([carry] * (span // 128), axis=1)
    carry_ref[...] = carry + total

    s = lax.dot_general(qbd_ref[...], kbuf_ref[...], (((1,), (1,)), ((), ())),
                        preferred_element_type=F32) + expand_heads(r)
    m_old = m_ref[...]
    m_new = jnp.maximum(m_old, jnp.max(s, axis=1, keepdims=True))
    alpha = jnp.exp(m_old - m_new)
    p = jnp.exp(s - m_new)
    m_ref[...] = m_new
    l_ref[...] = l_ref[...] * alpha + jnp.sum(p, axis=1, keepdims=True)
    acc_ref[...] = acc_ref[...] * alpha + jnp.dot(p.astype(BF16), vbuf_ref[...],
                                                  preferred_element_type=F32)

    @pl.when(step == pl.num_programs(1) - 1)
    def _():
        out = acc_ref[...] / l_ref[...]
        o_ref[...] = jnp.concatenate(
            [out[h * n_new:(h + 1) * n_new, h * HEAD_DIM:(h + 1) * HEAD_DIM] for h in range(N_HEADS)],
            axis=1)


def fox_sample(big, cn, cache_k, cache_v, cache_f_t, page_table, n_seq, n_new, tok0):
    n_pages = page_table.shape[1]
    pps = min(PAGES_PER_STEP, n_pages)
    n_steps = n_pages // pps
    n_rows = N_HEADS * n_new
    blk0 = tok0 // n_new

    def page_spec(shape, p_i):
        return pl.BlockSpec(
            shape, lambda b, j, pt, p_i=p_i: (pt[b * n_pages + (n_steps - 1 - j) * pps + p_i], 0, 0))

    def grp(g):
        return pl.BlockSpec((None, n_new, D_MODEL), lambda b, j, pt, g=g: (g, blk0 + b, 0))

    in_specs = ([page_spec((None, PAGE_SIZE, D_MODEL), p_i) for p_i in range(pps)]
                + [page_spec((None, PAGE_SIZE, D_MODEL), p_i) for p_i in range(pps)]
                + [page_spec((None, N_HEADS, PAGE_SIZE), p_i) for p_i in range(pps)]
                + [grp(G_QB), grp(G_KB), grp(G_VB),
                   pl.BlockSpec((None, N_HEADS, n_new), lambda b, j, pt: (b, 0, 0))])
    grid_spec = pltpu.PrefetchScalarGridSpec(
        num_scalar_prefetch=1,
        grid=(n_seq, n_steps),
        in_specs=in_specs,
        out_specs=pl.BlockSpec((n_new, D_MODEL), lambda b, j, pt: (b, 0)),
        scratch_shapes=[
            pltpu.VMEM((n_rows, D_MODEL), BF16),
            pltpu.VMEM((pps * PAGE_SIZE, D_MODEL), BF16),
            pltpu.VMEM((pps * PAGE_SIZE, D_MODEL), BF16),
            pltpu.VMEM((n_rows, 1), F32),
            pltpu.VMEM((n_rows, 1), F32),
            pltpu.VMEM((n_rows, D_MODEL), F32),
            pltpu.VMEM((N_HEADS, 128), F32),
        ],
    )
    args = [cache_k] * pps + [cache_v] * pps + [cache_f_t] * pps + [big, big, big, cn]
    return pl.pallas_call(
        functools.partial(_fox_sample_kernel, n_new=n_new, pps=pps),
        grid_spec=grid_spec,
        out_shape=jax.ShapeDtypeStruct((n_seq * n_new, D_MODEL), F32),
        compiler_params=_params(("arbitrary", "arbitrary"),40),
        name="fox_sample",
    )(page_table.reshape(-1), *args)


def _merge_kernel(x_ref, oap_ref, oas_ref, obp_ref, obs_ref, ga_ref, gb_ref, woa_ref, wob_ref,
                  wout_ref, nf_ref, wr_ref, br_ref, x1_ref, h2_ref, idx_ref, wts_ref, rank_ref,
                  cnt_ref, *, prompt_tiles):
    is_prompt = pl.program_id(0) < prompt_tiles
    oa = jnp.where(is_prompt, oap_ref[...], oas_ref[...])
    ob = jnp.where(is_prompt, obp_ref[...], obs_ref[...])
    ya = _bdot(oa, woa_ref[...])
    yb = _bdot(ob, wob_ref[...])
    mixed = _sigmoid(ga_ref[...]) * ya + _sigmoid(gb_ref[...]) * yb
    x1 = x_ref[...] + _bdot(mixed, wout_ref[...])
    x1_ref[...] = x1
    h2 = _rms(x1, nf_ref[...])
    h2_ref[...] = h2

    logits = jnp.dot(h2, wr_ref[...], preferred_element_type=F32,
                     precision=lax.Precision.HIGHEST) + br_ref[...]
    tm = logits.shape[0]
    lane = lax.broadcasted_iota(I32, logits.shape, 1).astype(F32)
    vals, idxs = [], []
    work = logits
    for _ in range(TOP_K):
        v = jnp.max(work, axis=1, keepdims=True)
        i = jnp.min(jnp.where(work == v, lane, float(ROUTER_LANES)), axis=1, keepdims=True)
        vals.append(v)
        idxs.append(i)
        work = jnp.where(lane == i, -jnp.inf, work)
    es = [jnp.exp(v - vals[0]) for v in vals]
    denom = es[0] + es[1] + es[2] + es[3]

    onehot = jnp.zeros(logits.shape, F32)
    for i in idxs:
        onehot = onehot + (lane == i).astype(F32)
    ti = lax.broadcasted_iota(I32, (tm, tm), 0)
    tj = lax.broadcasted_iota(I32, (tm, tm), 1)
    before = jnp.dot((tj < ti).astype(BF16), onehot.astype(BF16), preferred_element_type=F32)
    cnt_ref[...] = jnp.sum(onehot, axis=0, keepdims=True).astype(I32)

    l4 = lax.broadcasted_iota(I32, (tm, TOP_K), 1)
    idx_out = jnp.zeros((tm, TOP_K), I32)
    wts_out = jnp.zeros((tm, TOP_K), F32)
    rank_out = jnp.zeros((tm, TOP_K), I32)
    for k in range(TOP_K):
        rk = jnp.sum(jnp.where(lane == idxs[k], before, 0.0), axis=1, keepdims=True)
        idx_out = jnp.where(l4 == k, idxs[k].astype(I32), idx_out)
        wts_out = jnp.where(l4 == k, es[k] / denom, wts_out)
        rank_out = jnp.where(l4 == k, rk.astype(I32), rank_out)
    idx_ref[...] = idx_out
    wts_ref[...] = wts_out
    rank_ref[...] = rank_out


def merge(x, oa_p, oa_s, ob_p, ob_s, big, w_oa, w_ob, w_out, norm_ffn, w_router, b_router):
    n = x.shape[0]
    tm = _row_tile(oa_s.shape[0], _row_tile(oa_p.shape[0], MERGE_TILE))
    nt = n // tm
    ntp = oa_p.shape[0] // tm
    nts = oa_s.shape[0] // tm
    row = pl.BlockSpec((tm, D_MODEL), lambda i: (i, 0))
    row_p = pl.BlockSpec((tm, D_MODEL), lambda i: (jnp.minimum(i, ntp - 1), 0))
    row_s = pl.BlockSpec((tm, D_MODEL), lambda i: (jnp.clip(i - ntp, 0, nts - 1), 0))
    wfull = pl.BlockSpec((D_MODEL, D_MODEL), lambda i: (0, 0))
    k4 = pl.BlockSpec((tm, TOP_K), lambda i: (i, 0))
    return pl.pallas_call(
        functools.partial(_merge_kernel, prompt_tiles=ntp),
        grid=(nt,),
        in_specs=[
            row, row_p, row_s, row_p, row_s,
            pl.BlockSpec((None, tm, D_MODEL), lambda i: (G_GA, i, 0)),
            pl.BlockSpec((None, tm, D_MODEL), lambda i: (G_GB, i, 0)),
            wfull, wfull, wfull,
            pl.BlockSpec((1, D_MODEL), lambda i: (0, 0)),
            pl.BlockSpec((D_MODEL, ROUTER_LANES), lambda i: (0, 0)),
            pl.BlockSpec((1, ROUTER_LANES), lambda i: (0, 0)),
        ],
        out_specs=[row, row, k4, k4, k4, pl.BlockSpec((None, 1, ROUTER_LANES), lambda i: (i, 0, 0))],
        out_shape=[
            jax.ShapeDtypeStruct((n, D_MODEL), F32),
            jax.ShapeDtypeStruct((n, D_MODEL), F32),
            jax.ShapeDtypeStruct((n, TOP_K), I32),
            jax.ShapeDtypeStruct((n, TOP_K), F32),
            jax.ShapeDtypeStruct((n, TOP_K), I32),
            jax.ShapeDtypeStruct((nt, 1, ROUTER_LANES), I32),
        ],
        compiler_params=_params(("arbitrary",),56),
        name="merge",
    )(x, oa_p, oa_s, ob_p, ob_s, big, big, w_oa, w_ob, w_out, norm_ffn, w_router, b_router)


def _row_copy(src_ref, src_row, dst_ref, dst_row, sem):
    return pltpu.make_async_copy(src_ref.at[pl.ds(src_row, 1), :], dst_ref.at[pl.ds(dst_row, 1), :], sem)


def _dispatch_kernel(h_ref, dest_ref, xs_in_ref, xs_ref, sem):
    del xs_in_ref
    n_copies = dest_ref.shape[0]

    def start(a, carry):
        _row_copy(h_ref, lax.shift_right_logical(a, 2), xs_ref, dest_ref[a], sem).start()
        return carry

    def wait(a, carry):
        _row_copy(h_ref, 0, xs_ref, 0, sem).wait()
        return carry

    lax.fori_loop(0, n_copies, start, 0)
    lax.fori_loop(0, n_copies, wait, 0)


def dispatch(h2, dest_flat, n_rows):
    n = h2.shape[0]
    tm = _row_tile(n, GATHER_TILE)
    xs_init = jnp.zeros((n_rows, D_MODEL), F32)
    return pl.pallas_call(
        _dispatch_kernel,
        grid=(n // tm,),
        in_specs=[
            pl.BlockSpec((tm, D_MODEL), lambda i: (i, 0)),
            pl.BlockSpec((tm * TOP_K,), lambda i: (i,), memory_space=pltpu.SMEM),
            pl.BlockSpec(memory_space=pl.ANY),
        ],
        out_specs=pl.BlockSpec(memory_space=pl.ANY),
        out_shape=jax.ShapeDtypeStruct((n_rows, D_MODEL), F32),
        scratch_shapes=[pltpu.SemaphoreType.DMA(())],
        input_output_aliases={2: 0},
        compiler_params=_params(("arbitrary",), 16),
        name="dispatch",
    )(h2, dest_flat, xs_init)


def _moe_kernel(te_ref, nt_ref, x_ref, wg_ref, bg_ref, wu_ref, bu_ref, wd_ref, bd_ref, y_ref):
    del te_ref
    g = pl.program_id(0)

    @pl.when(g < nt_ref[0])
    def _():
        x = x_ref[...].astype(BF16)
        gt = jnp.minimum(jnp.dot(x, wg_ref[...], preferred_element_type=F32) + bg_ref[...], SWIGLU_LIMIT)
        up = jnp.clip(jnp.dot(x, wu_ref[...], preferred_element_type=F32) + bu_ref[...],
                      -SWIGLU_LIMIT, SWIGLU_LIMIT)
        act = (up + 1.0) * gt * _sigmoid(SWIGLU_ALPHA * gt)
        y_ref[...] = _bdot(act, wd_ref[...]) + bd_ref[...]

    @pl.when(g >= nt_ref[0])
    def _():
        y_ref[...] = jnp.zeros_like(y_ref)


def moe(xs, tile_expert, n_tiles, w_gate, b_gate, w_up, b_up, w_down, b_down):
    n_rows = xs.shape[0]
    tm = MOE_TILE
    wspec = pl.BlockSpec((None, D_MODEL, D_MODEL), lambda g, te, nt: (te[g], 0, 0))
    bspec = pl.BlockSpec((None, 1, D_MODEL), lambda g, te, nt: (te[g], 0, 0))
    row = pl.BlockSpec((tm, D_MODEL), lambda g, te, nt: (g, 0))
    grid_spec = pltpu.PrefetchScalarGridSpec(
        num_scalar_prefetch=2,
        grid=(n_rows // tm,),
        in_specs=[row, wspec, bspec, wspec, bspec, wspec, bspec],
        out_specs=row,
    )
    return pl.pallas_call(
        _moe_kernel,
        grid_spec=grid_spec,
        out_shape=jax.ShapeDtypeStruct((n_rows, D_MODEL), F32),
        compiler_params=_params(("arbitrary",), 48),
        name="moe",
    )(tile_expert, n_tiles, xs, w_gate, b_gate, w_up, b_up, w_down, b_down)


def _combine_kernel(x1_ref, wts_ref, pe_ref, dest_ref, ys_ref, np_ref, wpg_ref, wpp_ref, nfin_ref,
                    y_ref, buf_ref, sem):
    tm = x1_ref.shape[0]
    n_copies = tm * TOP_K

    def start(a, carry):
        _row_copy(ys_ref, dest_ref[a], buf_ref.at[a & (TOP_K - 1)], lax.shift_right_logical(a, 2),
                  sem).start()
        return carry

    def wait(a, carry):
        _row_copy(ys_ref, 0, buf_ref.at[0], 0, sem).wait()
        return carry

    lax.fori_loop(0, n_copies, start, 0)
    pe_proj = _bdot(pe_ref[...], wpp_ref[...])
    lax.fori_loop(0, n_copies, wait, 0)

    w = wts_ref[...]
    x2 = x1_ref[...]
    for k in range(TOP_K):
        x2 = x2 + w[:, k:k + 1] * buf_ref[k]
    gate = _sigmoid(_bdot(_rms(x2, np_ref[...]), wpg_ref[...]))
    x3 = x2 + pe_proj * gate
    y_ref[...] = _rms(x3, nfin_ref[...])


def combine(x1, wts, pe, dest_flat, ys, norm_ple, w_ple_gate, w_ple_proj, norm_final):
    n = x1.shape[0]
    tm = _row_tile(n, GATHER_TILE)
    ple = pe.shape[1]
    row = pl.BlockSpec((tm, D_MODEL), lambda i: (i, 0))
    vec = pl.BlockSpec((1, D_MODEL), lambda i: (0, 0))
    return pl.pallas_call(
        _combine_kernel,
        grid=(n // tm,),
        in_specs=[
            row,
            pl.BlockSpec((tm, TOP_K), lambda i: (i, 0)),
            pl.BlockSpec((tm, ple), lambda i: (i, 0)),
            pl.BlockSpec((tm * TOP_K,), lambda i: (i,), memory_space=pltpu.SMEM),
            pl.BlockSpec(memory_space=pl.ANY),
            vec,
            pl.BlockSpec((D_MODEL, D_MODEL), lambda i: (0, 0)),
            pl.BlockSpec((ple, D_MODEL), lambda i: (0, 0)),
            vec,
        ],
        out_specs=row,
        out_shape=jax.ShapeDtypeStruct((n, D_MODEL), F32),
        scratch_shapes=[pltpu.VMEM((TOP_K, tm, D_MODEL), F32), pltpu.SemaphoreType.DMA(())],
        compiler_params=_params(("arbitrary",), 32),
        name="combine",
    )(x1, wts, pe, dest_flat, ys, norm_ple, w_ple_gate, w_ple_proj, norm_final)


def _routing_tables(idx, rank, counts):
    n = idx.shape[0]
    tile = n // counts.shape[0]
    cnt = counts[:, 0, :N_EXPERTS]
    base = jnp.cumsum(cnt, axis=0) - cnt
    total = jnp.sum(cnt, axis=0)
    tiles_per_e = (total + MOE_TILE - 1) // MOE_TILE
    tile_end = jnp.cumsum(tiles_per_e)
    start_row = (tile_end - tiles_per_e) * MOE_TILE
    tok_tile = jnp.arange(n, dtype=I32)[:, None] // tile
    dest = start_row[idx] + base[tok_tile, idx] + rank
    n_rows = (pl.cdiv(n * TOP_K, MOE_TILE) + N_EXPERTS) * MOE_TILE
    g = jnp.arange(n_rows // MOE_TILE, dtype=I32)
    n_tiles = tile_end[-1].astype(I32)
    tile_expert = jnp.searchsorted(tile_end, jnp.minimum(g, n_tiles - 1), side="right").astype(I32)
    return dest.reshape(-1).astype(I32), tile_expert, n_tiles.reshape(1), n_rows


def kernel(x_prompt, x_sample, p_prompt, p_sample, cache_k, cache_v, cache_logf, state_S, state_conv,
           page_table, norm_mix, w_in, conv_w, a_log, dt_bias, norm_o_a, w_o_a, b_f, w_o_b, w_out,
           norm_ffn, w_router, b_router, w_gate, b_gate, w_up, b_up, w_down, b_down, norm_ple,
           w_ple_gate, w_ple_proj, norm_final):
    batch, seq, _ = x_prompt.shape
    n_seq, n_new, _ = x_sample.shape
    n_p = batch * seq
    n_s = n_seq * n_new
    n_pool = cache_k.shape[1]

    w = w_in[0]
    w_big = jnp.concatenate([w[:, 0:4096], w[:, 4112:7184], w[:, 7192:9240]], axis=1).astype(BF16)
    w_small = jnp.concatenate([w[:, 4096:4112], w[:, 7184:7192]], axis=1)
    w_small_t = jnp.pad(w_small.T, ((0, SMALL_ROWS - 24), (0, 0))).astype(BF16)
    gain_mix = norm_mix[0].reshape(1, D_MODEL)
    col = lambda v: v[0].reshape(N_HEADS, 1)
    w_oa_b, w_ob_b, w_out_b = (t[0].astype(BF16) for t in (w_o_a, w_o_b, w_out))
    wg_b, wu_b, wd_b = (t[0].astype(BF16) for t in (w_gate, w_up, w_down))
    bg, bu, bd = (t[0].reshape(N_EXPERTS, 1, D_MODEL) for t in (b_gate, b_up, b_down))
    w_pg_b = w_ple_gate[0].astype(BF16)
    w_pp_b = w_ple_proj[0].astype(BF16)

    w_router_p = jnp.pad(w_router[0], ((0, 0), (0, ROUTER_LANES - N_EXPERTS)))
    b_router_p = jnp.pad(b_router[0], (0, ROUTER_LANES - N_EXPERTS),
                         constant_values=-jnp.inf).reshape(1, ROUTER_LANES)

    cat = lambda a, b: jnp.concatenate([a, b], axis=0)
    x_all = cat(x_prompt.reshape(n_p, D_MODEL), x_sample.reshape(n_s, D_MODEL))
    pe_all = cat(p_prompt[0].reshape(n_p, -1), p_sample[0].reshape(n_s, -1))

    big, small_t = in_proj(x_all, gain_mix, w_big, w_small_t)
    small_p, small_s = small_t[:, :n_p], small_t[:, n_p:]

    lf_p, c_p = logf_scan(small_p, col(b_f), seq, seq)
    lf_s, c_s = logf_scan(small_s, col(b_f), n_new, n_s)

    def gdn_group(small_g, conv_state, s0, b, length, tok0):
        c = min(DN_CHUNK, length)
        small_c = small_g.reshape(SMALL_ROWS, b * length // c, c).transpose(1, 0, 2)
        conv_buf = jnp.pad(conv_state, ((0, 0), (8 - (CONV_W - 1), 0), (0, 0)))
        return gdn(big, small_c, conv_buf, s0, conv_w[0], col(a_log), col(dt_bias),
                   norm_o_a[0].reshape(1, HEAD_DIM), b, length, c, tok0)

    oa_p, s_p = gdn_group(small_p, jnp.zeros((batch, CONV_W - 1, CONV_CH), F32),
                          jnp.zeros((batch, N_HEADS, HEAD_DIM, HEAD_DIM), F32), batch, seq, 0)
    oa_s, s_s = gdn_group(small_s, state_conv[0], state_S[0], n_seq, n_new, n_p)

    ob_p = fox_prompt(big, c_p.reshape(N_HEADS, batch, 1, seq).transpose(1, 0, 2, 3), batch, seq)
    cn = c_s.reshape(N_HEADS, n_seq, n_new).transpose(1, 0, 2)
    cache_f_t = jnp.swapaxes(cache_logf[0], 1, 2)
    ob_s = fox_sample(big, cn, cache_k.reshape(n_pool, PAGE_SIZE, D_MODEL),
                      cache_v.reshape(n_pool, PAGE_SIZE, D_MODEL), cache_f_t, page_table,
                      n_seq, n_new, n_p)

    x1, h2, idx, wts, rank, cnt = merge(x_all, oa_p, oa_s, ob_p, ob_s, big, w_oa_b, w_ob_b, w_out_b,
                                        norm_ffn[0].reshape(1, D_MODEL), w_router_p, b_router_p)
    dest_flat, tile_expert, n_tiles, n_rows = _routing_tables(idx, rank, cnt)
    xs_sorted = dispatch(h2, dest_flat, n_rows)
    ys = moe(xs_sorted, tile_expert, n_tiles, wg_b, bg, wu_b, bu, wd_b, bd)
    y_all = combine(x1, wts, pe_all, dest_flat, ys, norm_ple[0].reshape(1, D_MODEL), w_pg_b, w_pp_b,
                    norm_final.reshape(1, D_MODEL))

    def states(lf_t, s_new, b, length, tok0):
        rows = lambda g: big[g, tok0:tok0 + b * length]
        head = lambda g: rows(g).reshape(1, b, length, N_HEADS, HEAD_DIM)
        last = lambda g: rows(g).reshape(b, length, D_MODEL)[:, length - (CONV_W - 1):, :]
        pre = jnp.concatenate([last(G_QA), last(G_KA), last(G_VA)], axis=2)
        return (head(G_KB), head(G_VB), lf_t.T.reshape(1, b, length, N_HEADS), s_new[None], pre[None])

    st_p = states(lf_p, s_p, batch, seq, 0)
    st_s = states(lf_s, s_s, n_seq, n_new, n_p)
    return (y_all[:n_p].reshape(batch, seq, D_MODEL), y_all[n_p:].reshape(n_seq, n_new, D_MODEL)) + st_p + st_s
```

```python
import functools

import jax
import jax.numpy as jnp
from jax import lax
from jax.experimental import pallas as pl
from jax.experimental.pallas import tpu as pltpu

F32 = jnp.float32
BF16 = jnp.bfloat16
I32 = jnp.int32

D_MODEL = 1024
N_HEADS = 8
HEAD_DIM = 128
CONV_W = 4
CONV_CH = 3 * D_MODEL
DN_CHUNK = 64
N_EXPERTS = 32
TOP_K = 4
PAGE_SIZE = 128
SWIGLU_LIMIT = 7.0
SWIGLU_ALPHA = 1.702
EPS = 1e-6
NEG_BIG = -1e30

G_QA, G_KA, G_VA, G_ZA, G_QB, G_KB, G_VB, G_GA, G_GB = range(9)
N_GROUPS = 9
SMALL_ROWS = 32

MIB = 1024 * 1024

ROUTER_LANES = 128

ROW_TILE = 512
MERGE_TILE = 256
MOE_TILE = 512
GATHER_TILE = 256
PAGES_PER_STEP = 8
DMA_UNROLL = 8


def _params(sem, vmem_mib):
    return pltpu.CompilerParams(dimension_semantics=sem, vmem_limit_bytes=vmem_mib * MIB)


def _row_tile(n, cap):
    t = cap
    while n % t:
        t //= 2
    assert t >= 8, (n, cap)
    return t


def _bdot(a, b):
    return jnp.dot(a.astype(BF16), b.astype(BF16), preferred_element_type=F32)


def _bdot_nt(a, b):
    return lax.dot_general(a.astype(BF16), b.astype(BF16), (((1,), (1,)), ((), ())),
                           preferred_element_type=F32)


def _bdot_tn(a, b):
    return lax.dot_general(a.astype(BF16), b.astype(BF16), (((0,), (0,)), ((), ())),
                           preferred_element_type=F32)


def _split3(x):
    hi = x.astype(BF16)
    r1 = x - hi.astype(F32)
    mid = r1.astype(BF16)
    lo = (r1 - mid.astype(F32)).astype(BF16)
    return hi, mid, lo


def _sigmoid(x):
    return jax.nn.sigmoid(x)


def _silu(x):
    return x * jax.nn.sigmoid(x)


def _softplus(x):
    return jnp.maximum(x, 0.0) + jnp.log1p(jnp.exp(-jnp.abs(x)))


def _log_sigmoid(x):
    return jnp.minimum(x, 0.0) - jnp.log1p(jnp.exp(-jnp.abs(x)))


def _rms(x, g):
    return x * lax.rsqrt(jnp.mean(x * x, axis=-1, keepdims=True) + EPS) * g


def _in_proj_kernel(x_ref, g_ref, w_ref, ws_ref, big_ref, st_ref, h_ref):
    @pl.when(pl.program_id(1) == 0)
    def _():
        hb = _rms(x_ref[...], g_ref[...]).astype(BF16)
        h_ref[...] = hb
        st_ref[...] = lax.dot_general(ws_ref[...], hb, (((1,), (1,)), ((), ())),
                                      preferred_element_type=F32)

    big_ref[...] = jnp.dot(h_ref[...], w_ref[...], preferred_element_type=F32)


def in_proj(x, gain, w_big, w_small_t):
    n = x.shape[0]
    tm = _row_tile(n, ROW_TILE)
    return pl.pallas_call(
        _in_proj_kernel,
        grid=(n // tm, N_GROUPS),
        in_specs=[
            pl.BlockSpec((tm, D_MODEL), lambda i, g: (i, 0)),
            pl.BlockSpec((1, D_MODEL), lambda i, g: (0, 0)),
            pl.BlockSpec((D_MODEL, D_MODEL), lambda i, g: (0, g)),
            pl.BlockSpec((SMALL_ROWS, D_MODEL), lambda i, g: (0, 0)),
        ],
        out_specs=[
            pl.BlockSpec((None, tm, D_MODEL), lambda i, g: (g, i, 0)),
            pl.BlockSpec((SMALL_ROWS, tm), lambda i, g: (0, i)),
        ],
        out_shape=[
            jax.ShapeDtypeStruct((N_GROUPS, n, D_MODEL), F32),
            jax.ShapeDtypeStruct((SMALL_ROWS, n), F32),
        ],
        scratch_shapes=[pltpu.VMEM((tm, D_MODEL), BF16)],
        compiler_params=_params(("arbitrary", "arbitrary"),32),
        name="in_proj",
    )(x, gain, w_big, w_small_t)


def _logf_kernel(st_ref, bf_ref, lf_ref, c_ref, *, seg):
    lf = _log_sigmoid(st_ref[16:24, :] + bf_ref[...])
    lf_ref[...] = lf
    pos = lax.broadcasted_iota(I32, lf.shape, 1) % seg
    y = lf
    k = 1
    while k < seg:
        y = y + jnp.where(pos >= k, pltpu.roll(y, k, axis=1), 0.0)
        k *= 2
    c_ref[...] = y


def logf_scan(small_t, b_f, seg, block):
    n = small_t.shape[1]
    return pl.pallas_call(
        functools.partial(_logf_kernel, seg=seg),
        grid=(n // block,),
        in_specs=[
            pl.BlockSpec((SMALL_ROWS, block), lambda i: (0, i)),
            pl.BlockSpec((N_HEADS, 1), lambda i: (0, 0)),
        ],
        out_specs=[
            pl.BlockSpec((N_HEADS, block), lambda i: (0, i)),
            pl.BlockSpec((N_HEADS, block), lambda i: (0, i)),
        ],
        out_shape=[jax.ShapeDtypeStruct((N_HEADS, n), F32)] * 2,
        compiler_params=_params(("arbitrary",),16),
        name="logf_scan",
    )(small_t, b_f)


def _unit_lower_inverse(m, c, ii, jj, eye):
    def same_block(s):
        return (ii // s) == (jj // s)

    hs = range(len(m))
    d = [jnp.where(same_block(8), m[h], 0.0) for h in hs]
    d2 = [_bdot(d[h], d[h]) for h in hs]
    d4 = [_bdot(d2[h], d2[h]) for h in hs]
    x = [eye - d[h] for h in hs]
    x = [x[h] + _bdot(x[h], d2[h]) for h in hs]
    x = [x[h] + _bdot(x[h], d4[h]) for h in hs]
    s = 8
    while s < c:
        sel = same_block(2 * s) & jnp.logical_not(same_block(s))
        t = [_bdot(x[h], jnp.where(sel, m[h], 0.0)) for h in hs]
        x = [x[h] - _bdot(t[h], x[h]) for h in hs]
        s *= 2
    return x


def _gdn_kernel(q_ref, k_ref, v_ref, z_ref, sm_ref, cb_ref, s0_ref, cw_ref, al_ref, dt_ref, no_ref,
                o_ref, s_ref, full_ref, *, c):
    n = pl.program_id(1)

    @pl.when(n == 0)
    def _():
        full_ref[0:8, :] = cb_ref[...]
        s_ref[...] = s0_ref[...]

    full_ref[8:8 + c, 0:D_MODEL] = q_ref[...]
    full_ref[8:8 + c, D_MODEL:2 * D_MODEL] = k_ref[...]
    full_ref[8:8 + c, 2 * D_MODEL:3 * D_MODEL] = v_ref[...]

    conv = cw_ref[0:1, :] * full_ref[5:5 + c, :]
    for i in range(1, CONV_W):
        conv = conv + cw_ref[i:i + 1, :] * full_ref[5 + i:5 + i + c, :]
    u = _silu(conv)
    tail = full_ref[c:c + 8, :]
    full_ref[0:8, :] = tail

    sm = sm_ref[...]
    beta_all = _sigmoid(sm[0:8, :])
    g_all = -jnp.exp(al_ref[...]) * _softplus(sm[8:16, :] + dt_ref[...])

    ii = lax.broadcasted_iota(I32, (c, c), 0)
    jj = lax.broadcasted_iota(I32, (c, c), 1)
    incl = ii >= jj
    strict = ii > jj
    eye = (ii == jj).astype(F32)
    lower_b = incl.astype(BF16)
    upper_b = (ii <= jj).astype(BF16)
    eye_b = eye.astype(BF16)

    g_row = jnp.zeros((N_HEADS, c), F32)
    g_col = jnp.zeros((c, N_HEADS), F32)
    beta_col = jnp.zeros((c, N_HEADS), F32)
    for gp, bp in zip(_split3(g_all), _split3(beta_all)):
        g_row = g_row + jnp.dot(gp, upper_b, preferred_element_type=F32)
        g_col = g_col + lax.dot_general(lower_b, gp, (((1,), (1,)), ((), ())),
                                        preferred_element_type=F32)
        beta_col = beta_col + lax.dot_general(eye_b, bp, (((1,), (1,)), ((), ())),
                                              preferred_element_type=F32)

    hs = range(N_HEADS)
    sls = [slice(h * HEAD_DIM, (h + 1) * HEAD_DIM) for h in hs]
    qh, kh, vh = [], [], []
    for h in hs:
        q_ = u[:, h * HEAD_DIM:(h + 1) * HEAD_DIM]
        k_ = u[:, D_MODEL + h * HEAD_DIM:D_MODEL + (h + 1) * HEAD_DIM]
        qh.append(q_ * lax.rsqrt(jnp.sum(q_ * q_, axis=-1, keepdims=True) + EPS) * (HEAD_DIM ** -0.5))
        kh.append(k_ * lax.rsqrt(jnp.sum(k_ * k_, axis=-1, keepdims=True) + EPS))
        vh.append(u[:, 2 * D_MODEL + h * HEAD_DIM:2 * D_MODEL + (h + 1) * HEAD_DIM])
    gc = [g_col[:, h:h + 1] for h in hs]
    gr = [g_row[h:h + 1, :] for h in hs]
    bc = [beta_col[:, h:h + 1] for h in hs]
    decay = [jnp.where(incl, jnp.exp(jnp.where(incl, gc[h] - gr[h], 0.0)), 0.0) for h in hs]
    kb = [kh[h] * bc[h] for h in hs]
    kk = [_bdot_nt(kb[h], kh[h]) for h in hs]
    qk = [_bdot_nt(qh[h], kh[h]) * decay[h] for h in hs]
    m = [jnp.where(strict, kk[h] * decay[h], 0.0) for h in hs]
    eg = [jnp.exp(gc[h]) for h in hs]
    rhs = [jnp.concatenate([vh[h] * bc[h], kb[h] * eg[h]], axis=1) for h in hs]
    inv = _unit_lower_inverse(m, c, ii, jj, eye)
    sol = [_bdot(inv[h], rhs[h]) for h in hs]
    g_last = [gr[h][:, c - 1:c] for h in hs]
    q_dec = [qh[h] * eg[h] for h in hs]
    k_dec = [kh[h] * jnp.exp(g_last[h] - gc[h]) for h in hs]

    s_old = [s_ref[h] for h in hs]
    w_s = [_bdot(sol[h][:, HEAD_DIM:], s_old[h]) for h in hs]
    q_s = [_bdot(q_dec[h], s_old[h]) for h in hs]
    uu = [sol[h][:, :HEAD_DIM] - w_s[h] for h in hs]
    o = [q_s[h] + _bdot(qk[h], uu[h]) for h in hs]
    s_upd = [_bdot_tn(k_dec[h], uu[h]) for h in hs]
    for h in hs:
        s_ref[h] = s_old[h] * jnp.exp(g_last[h]) + s_upd[h]
        o_ref[:, sls[h]] = _rms(o[h], no_ref[...]) * _silu(z_ref[:, sls[h]])


def gdn(big, small_c, conv_buf, s0, conv_w, a_log, dt_bias, norm_o, batch, length, c, tok0):
    n_tok = batch * length
    nc = length // c
    blk0 = tok0 // c

    def grp(g):
        return pl.BlockSpec((None, c, D_MODEL), lambda b, n, g=g: (g, blk0 + b * nc + n, 0))

    def full(shape):
        return pl.BlockSpec(shape, lambda b, n: (0,) * len(shape))

    return pl.pallas_call(
        functools.partial(_gdn_kernel, c=c),
        grid=(batch, nc),
        in_specs=[
            grp(G_QA), grp(G_KA), grp(G_VA), grp(G_ZA),
            pl.BlockSpec((None, SMALL_ROWS, c), lambda b, n: (b * nc + n, 0, 0)),
            pl.BlockSpec((None, 8, CONV_CH), lambda b, n: (b, 0, 0)),
            pl.BlockSpec((None, N_HEADS, HEAD_DIM, HEAD_DIM), lambda b, n: (b, 0, 0, 0)),
            full((CONV_W, CONV_CH)), full((N_HEADS, 1)), full((N_HEADS, 1)), full((1, HEAD_DIM)),
        ],
        out_specs=[
            pl.BlockSpec((c, D_MODEL), lambda b, n: (b * nc + n, 0)),
            pl.BlockSpec((None, N_HEADS, HEAD_DIM, HEAD_DIM), lambda b, n: (b, 0, 0, 0)),
        ],
        out_shape=[
            jax.ShapeDtypeStruct((n_tok, D_MODEL), F32),
            jax.ShapeDtypeStruct((batch, N_HEADS, HEAD_DIM, HEAD_DIM), F32),
        ],
        scratch_shapes=[pltpu.VMEM((c + 8, CONV_CH), F32)],
        compiler_params=_params(("arbitrary", "arbitrary"),32),
        name="gdn",
    )(big, big, big, big, small_c, conv_buf, s0, conv_w, a_log, dt_bias, norm_o)


def _fox_prompt_kernel(q_ref, k_ref, v_ref, c_ref, o_ref, *, tq, tk):
    qi = pl.program_id(2)
    q = (q_ref[...] * (HEAD_DIM ** -0.5)).astype(BF16)
    q0 = pl.multiple_of(qi * tq, tq)
    c_base = jnp.max(c_ref[:, pl.ds(q0, 128)], axis=1, keepdims=True)
    rows = q0 + lax.broadcasted_iota(I32, (tq, tk), 0)
    cols = lax.broadcasted_iota(I32, (tq, tk), 1)

    def body(ki, carry):
        m, l, acc = carry
        k0 = pl.multiple_of(ki * tk, tk)
        kb = k_ref[pl.ds(k0, tk), :].astype(BF16)
        vb = v_ref[pl.ds(k0, tk), :].astype(BF16)
        bias = c_base - c_ref[:, pl.ds(k0, tk)]
        s = lax.dot_general(q, kb, (((1,), (1,)), ((), ())), preferred_element_type=F32) + bias
        s = jnp.where(cols + k0 <= rows, s, NEG_BIG)
        m_new = jnp.maximum(m, jnp.max(s, axis=1, keepdims=True))
        alpha = jnp.exp(m - m_new)
        p = jnp.exp(s - m_new)
        l = l * alpha + jnp.sum(p, axis=1, keepdims=True)
        acc = acc * alpha + jnp.dot(p.astype(BF16), vb, preferred_element_type=F32)
        return m_new, l, acc

    init = (jnp.full((tq, 1), NEG_BIG, F32), jnp.zeros((tq, 1), F32), jnp.zeros((tq, HEAD_DIM), F32))
    n_kv = (q0 + tq + tk - 1) // tk
    _, l, acc = lax.fori_loop(0, n_kv, body, init)
    o_ref[...] = acc / l


def fox_prompt(big, c_t, batch, seq):
    tq = min(512, seq)
    tk = tq
    nq = seq // tq

    def grp(g):
        return pl.BlockSpec((None, seq, HEAD_DIM), lambda b, h, qi, g=g: (g, b, h))

    return pl.pallas_call(
        functools.partial(_fox_prompt_kernel, tq=tq, tk=tk),
        grid=(batch, N_HEADS, nq),
        in_specs=[
            pl.BlockSpec((None, tq, HEAD_DIM), lambda b, h, qi: (G_QB, b * nq + qi, h)),
            grp(G_KB), grp(G_VB),
            pl.BlockSpec((None, None, 1, seq), lambda b, h, qi: (b, h, 0, 0)),
        ],
        out_specs=pl.BlockSpec((tq, HEAD_DIM), lambda b, h, qi: (b * nq + qi, h)),
        out_shape=jax.ShapeDtypeStruct((batch * seq, D_MODEL), F32),
        compiler_params=_params(("arbitrary", "arbitrary", "arbitrary"), 32),
        name="fox_prompt",
    )(big, big, big, c_t)


def _fox_sample_kernel(pt_ref, *refs, n_new, pps):
    del pt_ref
    k_pages = refs[0:pps]
    v_pages = refs[pps:2 * pps]
    f_pages = refs[2 * pps:3 * pps]
    q_ref, kn_ref, vn_ref, cn_ref = refs[3 * pps:3 * pps + 4]
    o_ref = refs[3 * pps + 4]
    qf_ref, m_ref, l_ref, acc_ref, carry_ref = refs[3 * pps + 5:]
    step = pl.program_id(1)
    n_rows = N_HEADS * n_new
    page_cols = PAGE_SIZE * N_HEADS

    def by_head(x):
        return jnp.concatenate([x[:, h * HEAD_DIM:(h + 1) * HEAD_DIM] for h in range(N_HEADS)], axis=0)

    @pl.when(step == 0)
    def _():
        qf = by_head(q_ref[...] * (HEAD_DIM ** -0.5)).astype(BF16)
        qf_ref[...] = qf
        s = _bdot_nt(qf, by_head(kn_ref[...])) - cn_ref[...]
        r_i = lax.broadcasted_iota(I32, (n_rows, n_rows), 0)
        c_i = lax.broadcasted_iota(I32, (n_rows, n_rows), 1)
        ok = (r_i // n_new == c_i // n_new) & (c_i % n_new <= r_i % n_new)
        s = jnp.where(ok, s, NEG_BIG)
        m = jnp.max(s, axis=1, keepdims=True)
        p = jnp.exp(s - m)
        m_ref[...] = m
        l_ref[...] = jnp.sum(p, axis=1, keepdims=True)
        acc_ref[...] = _bdot(p, by_head(vn_ref[...]))
        carry_ref[...] = jnp.zeros_like(carry_ref)

    lf = jnp.concatenate([f_pages[p_i][...] for p_i in range(pps)], axis=0)
    lane = lax.broadcasted_iota(I32, lf.shape, 1)
    row = lax.broadcasted_iota(I32, lf.shape, 0)
    y = lf
    tot = lf
    k = N_HEADS
    while k < page_cols:
        y = y + jnp.where(lane + k < page_cols, pltpu.roll(y, page_cols - k, axis=1), 0.0)
        tot = tot + pltpu.roll(tot, page_cols - k, axis=1)
        k *= 2
    later = tot
    k = 1
    while k < pps:
        later = later + jnp.where(row + k < pps, pltpu.roll(later, pps - k, axis=0), 0.0)
        k *= 2
    carry = carry_ref[...]
    r = (y - lf) + (later - tot) + carry
    carry_ref[...] = carry + later[0:1, :]

    r_h = lax.broadcasted_iota(I32, (n_rows, page_cols), 0) // n_new
    c_h = lax.broadcasted_iota(I32, (n_rows, page_cols), 1) % N_HEADS
    same_head = r_h == c_h
    qf = qf_ref[...]
    scores = []
    for p_i in range(pps):
        k2 = k_pages[p_i][...].reshape(page_cols, HEAD_DIM)
        s_p = _bdot_nt(qf, k2) + r[p_i:p_i + 1, :]
        scores.append(jnp.where(same_head, s_p, NEG_BIG))
    s = jnp.concatenate(scores, axis=1)
    m_old = m_ref[...]
    m_new = jnp.maximum(m_old, jnp.max(s, axis=1, keepdims=True))
    alpha = jnp.exp(m_old - m_new)
    p = jnp.exp(s - m_new)
    m_ref[...] = m_new
    l_ref[...] = l_ref[...] * alpha + jnp.sum(p, axis=1, keepdims=True)
    pv = jnp.zeros((n_rows, HEAD_DIM), F32)
    for p_i in range(pps):
        v2 = v_pages[p_i][...].reshape(page_cols, HEAD_DIM)
        pv = pv + _bdot(p[:, p_i * page_cols:(p_i + 1) * page_cols], v2)
    acc_ref[...] = acc_ref[...] * alpha + pv

    @pl.when(step == pl.num_programs(1) - 1)
    def _():
        out = acc_ref[...] / l_ref[...]
        o_ref[...] = jnp.concatenate([out[h * n_new:(h + 1) * n_new, :] for h in range(N_HEADS)], axis=1)


def fox_sample(big, cn, cache_k, cache_v, cache_f, page_table, n_seq, n_new, tok0):
    n_pages = page_table.shape[1]
    pps = min(PAGES_PER_STEP, n_pages)
    n_steps = n_pages // pps
    n_rows = N_HEADS * n_new
    page_cols = PAGE_SIZE * N_HEADS
    blk0 = tok0 // n_new

    def page_spec(shape, p_i):
        return pl.BlockSpec(shape, lambda b, j, pt, p_i=p_i: (
            (pt[b * n_pages + (n_steps - 1 - j) * pps + p_i],) + (0,) * (len(shape) - 1)))

    def grp(g):
        return pl.BlockSpec((None, n_new, D_MODEL), lambda b, j, pt, g=g: (g, blk0 + b, 0))

    kv_shape = (None, PAGE_SIZE, N_HEADS, HEAD_DIM)
    in_specs = ([page_spec(kv_shape, p_i) for p_i in range(pps)]
                + [page_spec(kv_shape, p_i) for p_i in range(pps)]
                + [page_spec((None, 1, page_cols), p_i) for p_i in range(pps)]
                + [grp(G_QB), grp(G_KB), grp(G_VB),
                   pl.BlockSpec((None, 1, n_rows), lambda b, j, pt: (b, 0, 0))])
    grid_spec = pltpu.PrefetchScalarGridSpec(
        num_scalar_prefetch=1,
        grid=(n_seq, n_steps),
        in_specs=in_specs,
        out_specs=pl.BlockSpec((n_new, D_MODEL), lambda b, j, pt: (b, 0)),
        scratch_shapes=[
            pltpu.VMEM((n_rows, HEAD_DIM), BF16),
            pltpu.VMEM((n_rows, 1), F32),
            pltpu.VMEM((n_rows, 1), F32),
            pltpu.VMEM((n_rows, HEAD_DIM), F32),
            pltpu.VMEM((1, page_cols), F32),
        ],
    )
    args = [cache_k] * pps + [cache_v] * pps + [cache_f] * pps + [big, big, big, cn]
    return pl.pallas_call(
        functools.partial(_fox_sample_kernel, n_new=n_new, pps=pps),
        grid_spec=grid_spec,
        out_shape=jax.ShapeDtypeStruct((n_seq * n_new, D_MODEL), F32),
        compiler_params=_params(("arbitrary", "arbitrary"),40),
        name="fox_sample",
    )(page_table.reshape(-1), *args)


def _merge_kernel(x_ref, oap_ref, oas_ref, obp_ref, obs_ref, ga_ref, gb_ref, woa_ref, wob_ref,
                  wout_ref, nf_ref, wr_ref, br_ref, x1_ref, h2_ref, idx_ref, wts_ref, rank_ref,
                  cnt_ref, *, prompt_tiles):
    is_prompt = pl.program_id(0) < prompt_tiles
    oa = jnp.where(is_prompt, oap_ref[...], oas_ref[...])
    ob = jnp.where(is_prompt, obp_ref[...], obs_ref[...])
    ya = _bdot(oa, woa_ref[...])
    yb = _bdot(ob, wob_ref[...])
    mixed = _sigmoid(ga_ref[...]) * ya + _sigmoid(gb_ref[...]) * yb
    x1 = x_ref[...] + _bdot(mixed, wout_ref[...])
    x1_ref[...] = x1
    h2 = _rms(x1, nf_ref[...])
    h2_ref[...] = h2

    logits = jnp.dot(h2, wr_ref[...], preferred_element_type=F32,
                     precision=lax.Precision.HIGHEST) + br_ref[...]
    tm = logits.shape[0]
    lane = lax.broadcasted_iota(I32, logits.shape, 1).astype(F32)
    vals, idxs = [], []
    work = logits
    for _ in range(TOP_K):
        v = jnp.max(work, axis=1, keepdims=True)
        i = jnp.min(jnp.where(work == v, lane, float(ROUTER_LANES)), axis=1, keepdims=True)
        vals.append(v)
        idxs.append(i)
        work = jnp.where(lane == i, -jnp.inf, work)
    es = [jnp.exp(v - vals[0]) for v in vals]
    denom = es[0] + es[1] + es[2] + es[3]

    onehot = jnp.zeros(logits.shape, F32)
    for i in idxs:
        onehot = onehot + (lane == i).astype(F32)
    ti = lax.broadcasted_iota(I32, (tm, tm), 0)
    tj = lax.broadcasted_iota(I32, (tm, tm), 1)
    before = jnp.dot((tj < ti).astype(BF16), onehot.astype(BF16), preferred_element_type=F32)
    cnt_ref[...] = jnp.sum(onehot, axis=0, keepdims=True).astype(I32)

    l4 = lax.broadcasted_iota(I32, (tm, TOP_K), 1)
    idx_out = jnp.zeros((tm, TOP_K), I32)
    wts_out = jnp.zeros((tm, TOP_K), F32)
    rank_out = jnp.zeros((tm, TOP_K), I32)
    for k in range(TOP_K):
        rk = jnp.sum(jnp.where(lane == idxs[k], before, 0.0), axis=1, keepdims=True)
        idx_out = jnp.where(l4 == k, idxs[k].astype(I32), idx_out)
        wts_out = jnp.where(l4 == k, es[k] / denom, wts_out)
        rank_out = jnp.where(l4 == k, rk.astype(I32), rank_out)
    idx_ref[...] = idx_out
    wts_ref[...] = wts_out
    rank_ref[...] = rank_out


def merge(x, oa_p, oa_s, ob_p, ob_s, big, w_oa, w_ob, w_out, norm_ffn, w_router, b_router):
    n = x.shape[0]
    tm = _row_tile(oa_s.shape[0], _row_tile(oa_p.shape[0], MERGE_TILE))
    nt = n // tm
    ntp = oa_p.shape[0] // tm
    nts = oa_s.shape[0] // tm
    row = pl.BlockSpec((tm, D_MODEL), lambda i: (i, 0))
    row_p = pl.BlockSpec((tm, D_MODEL), lambda i: (jnp.minimum(i, ntp - 1), 0))
    row_s = pl.BlockSpec((tm, D_MODEL), lambda i: (jnp.clip(i - ntp, 0, nts - 1), 0))
    wfull = pl.BlockSpec((D_MODEL, D_MODEL), lambda i: (0, 0))
    k4 = pl.BlockSpec((tm, TOP_K), lambda i: (i, 0))
    return pl.pallas_call(
        functools.partial(_merge_kernel, prompt_tiles=ntp),
        grid=(nt,),
        in_specs=[
            row, row_p, row_s, row_p, row_s,
            pl.BlockSpec((None, tm, D_MODEL), lambda i: (G_GA, i, 0)),
            pl.BlockSpec((None, tm, D_MODEL), lambda i: (G_GB, i, 0)),
            wfull, wfull, wfull,
            pl.BlockSpec((1, D_MODEL), lambda i: (0, 0)),
            pl.BlockSpec((D_MODEL, ROUTER_LANES), lambda i: (0, 0)),
            pl.BlockSpec((1, ROUTER_LANES), lambda i: (0, 0)),
        ],
        out_specs=[row, row, k4, k4, k4, pl.BlockSpec((None, 1, ROUTER_LANES), lambda i: (i, 0, 0))],
        out_shape=[
            jax.ShapeDtypeStruct((n, D_MODEL), F32),
            jax.ShapeDtypeStruct((n, D_MODEL), F32),
            jax.ShapeDtypeStruct((n, TOP_K), I32),
            jax.ShapeDtypeStruct((n, TOP_K), F32),
            jax.ShapeDtypeStruct((n, TOP_K), I32),
            jax.ShapeDtypeStruct((nt, 1, ROUTER_LANES), I32),
        ],
        compiler_params=_params(("arbitrary",),56),
        name="merge",
    )(x, oa_p, oa_s, ob_p, ob_s, big, big, w_oa, w_ob, w_out, norm_ffn, w_router, b_router)


def _row_copy(src_ref, src_row, dst_ref, dst_row, sem):
    return pltpu.make_async_copy(src_ref.at[pl.ds(src_row, 1), :], dst_ref.at[pl.ds(dst_row, 1), :], sem)


def _dispatch_kernel(h_ref, dest_ref, xs_in_ref, xs_ref, sem):
    del xs_in_ref
    n_copies = dest_ref.shape[0]

    tm = h_ref.shape[0]

    def start(a, carry):
        _row_copy(h_ref, lax.shift_right_logical(a, 2), xs_ref, dest_ref[a], sem).start()
        return carry

    lax.fori_loop(0, n_copies, start, 0, unroll=DMA_UNROLL)
    for _ in range(TOP_K):
        pltpu.make_async_copy(h_ref, xs_ref.at[pl.ds(0, tm), :], sem).wait()


def dispatch(h2, dest_flat, n_rows):
    n = h2.shape[0]
    tm = _row_tile(n, GATHER_TILE)
    xs_init = jnp.zeros((n_rows, D_MODEL), F32)
    return pl.pallas_call(
        _dispatch_kernel,
        grid=(n // tm,),
        in_specs=[
            pl.BlockSpec((tm, D_MODEL), lambda i: (i, 0)),
            pl.BlockSpec((tm * TOP_K,), lambda i: (i,), memory_space=pltpu.SMEM),
            pl.BlockSpec(memory_space=pl.ANY),
        ],
        out_specs=pl.BlockSpec(memory_space=pl.ANY),
        out_shape=jax.ShapeDtypeStruct((n_rows, D_MODEL), F32),
        scratch_shapes=[pltpu.SemaphoreType.DMA(())],
        input_output_aliases={2: 0},
        compiler_params=_params(("arbitrary",), 16),
        name="dispatch",
    )(h2, dest_flat, xs_init)


def _moe_kernel(te_ref, nt_ref, x_ref, wg_ref, bg_ref, wu_ref, bu_ref, wd_ref, bd_ref, y_ref):
    del te_ref
    g = pl.program_id(0)

    @pl.when(g < nt_ref[0])
    def _():
        x = x_ref[...].astype(BF16)
        gt = jnp.minimum(jnp.dot(x, wg_ref[...], preferred_element_type=F32) + bg_ref[...], SWIGLU_LIMIT)
        up = jnp.clip(jnp.dot(x, wu_ref[...], preferred_element_type=F32) + bu_ref[...],
                      -SWIGLU_LIMIT, SWIGLU_LIMIT)
        act = (up + 1.0) * gt * _sigmoid(SWIGLU_ALPHA * gt)
        y_ref[...] = _bdot(act, wd_ref[...]) + bd_ref[...]

    @pl.when(g >= nt_ref[0])
    def _():
        y_ref[...] = jnp.zeros_like(y_ref)


def moe(xs, tile_expert, n_tiles, w_gate, b_gate, w_up, b_up, w_down, b_down):
    n_rows = xs.shape[0]
    tm = MOE_TILE
    wspec = pl.BlockSpec((None, D_MODEL, D_MODEL), lambda g, te, nt: (te[g], 0, 0))
    bspec = pl.BlockSpec((None, 1, D_MODEL), lambda g, te, nt: (te[g], 0, 0))
    row = pl.BlockSpec((tm, D_MODEL), lambda g, te, nt: (g, 0))
    grid_spec = pltpu.PrefetchScalarGridSpec(
        num_scalar_prefetch=2,
        grid=(n_rows // tm,),
        in_specs=[row, wspec, bspec, wspec, bspec, wspec, bspec],
        out_specs=row,
    )
    return pl.pallas_call(
        _moe_kernel,
        grid_spec=grid_spec,
        out_shape=jax.ShapeDtypeStruct((n_rows, D_MODEL), F32),
        compiler_params=_params(("arbitrary",), 48),
        name="moe",
    )(tile_expert, n_tiles, xs, w_gate, b_gate, w_up, b_up, w_down, b_down)


def _combine_kernel(x1_ref, wts_ref, pe_ref, dest_ref, ys_ref, np_ref, wpg_ref, wpp_ref, nfin_ref,
                    y_ref, buf_ref, sem):
    tm = x1_ref.shape[0]
    n_copies = tm * TOP_K

    def start(a, carry):
        _row_copy(ys_ref, dest_ref[a], buf_ref.at[a & (TOP_K - 1)], lax.shift_right_logical(a, 2),
                  sem).start()
        return carry

    lax.fori_loop(0, n_copies, start, 0, unroll=DMA_UNROLL)
    pe_proj = _bdot(pe_ref[...], wpp_ref[...])
    for k in range(TOP_K):
        pltpu.make_async_copy(ys_ref.at[pl.ds(0, tm), :], buf_ref.at[k], sem).wait()

    w = wts_ref[...]
    x2 = x1_ref[...]
    for k in range(TOP_K):
        x2 = x2 + w[:, k:k + 1] * buf_ref[k]
    gate = _sigmoid(_bdot(_rms(x2, np_ref[...]), wpg_ref[...]))
    x3 = x2 + pe_proj * gate
    y_ref[...] = _rms(x3, nfin_ref[...])


def combine(x1, wts, pe, dest_flat, ys, norm_ple, w_ple_gate, w_ple_proj, norm_final):
    n = x1.shape[0]
    tm = _row_tile(n, GATHER_TILE)
    ple = pe.shape[1]
    row = pl.BlockSpec((tm, D_MODEL), lambda i: (i, 0))
    vec = pl.BlockSpec((1, D_MODEL), lambda i: (0, 0))
    return pl.pallas_call(
        _combine_kernel,
        grid=(n // tm,),
        in_specs=[
            row,
            pl.BlockSpec((tm, TOP_K), lambda i: (i, 0)),
            pl.BlockSpec((tm, ple), lambda i: (i, 0)),
            pl.BlockSpec((tm * TOP_K,), lambda i: (i,), memory_space=pltpu.SMEM),
            pl.BlockSpec(memory_space=pl.ANY),
            vec,
            pl.BlockSpec((D_MODEL, D_MODEL), lambda i: (0, 0)),
            pl.BlockSpec((ple, D_MODEL), lambda i: (0, 0)),
            vec,
        ],
        out_specs=row,
        out_shape=jax.ShapeDtypeStruct((n, D_MODEL), F32),
        scratch_shapes=[pltpu.VMEM((TOP_K, tm, D_MODEL), F32), pltpu.SemaphoreType.DMA(())],
        compiler_params=_params(("arbitrary",), 32),
        name="combine",
    )(x1, wts, pe, dest_flat, ys, norm_ple, w_ple_gate, w_ple_proj, norm_final)


def _routing_tables(idx, rank, counts):
    n = idx.shape[0]
    tile = n // counts.shape[0]
    cnt = counts[:, 0, :N_EXPERTS]
    base = jnp.cumsum(cnt, axis=0) - cnt
    total = jnp.sum(cnt, axis=0)
    tiles_per_e = (total + MOE_TILE - 1) // MOE_TILE
    tile_end = jnp.cumsum(tiles_per_e)
    start_row = (tile_end - tiles_per_e) * MOE_TILE
    offs = (start_row[None, :] + base).astype(I32)
    offs_tok = jnp.broadcast_to(offs[:, None, None, :], (n // tile, tile, 1, N_EXPERTS))
    onehot = idx.reshape(n // tile, tile, TOP_K, 1) == jnp.arange(N_EXPERTS, dtype=I32)
    dest = jnp.sum(jnp.where(onehot, offs_tok, 0), axis=-1).reshape(n, TOP_K) + rank
    n_rows = (pl.cdiv(n * TOP_K, MOE_TILE) + N_EXPERTS) * MOE_TILE
    g = jnp.arange(n_rows // MOE_TILE, dtype=I32)
    n_tiles = tile_end[-1].astype(I32)
    g_eff = jnp.minimum(g, n_tiles - 1)
    tile_expert = jnp.sum((g_eff[:, None] >= tile_end[None, :]).astype(I32), axis=1)
    return dest.reshape(-1).astype(I32), tile_expert, n_tiles.reshape(1), n_rows


def kernel(x_prompt, x_sample, p_prompt, p_sample, cache_k, cache_v, cache_logf, state_S, state_conv,
           page_table, norm_mix, w_in, conv_w, a_log, dt_bias, norm_o_a, w_o_a, b_f, w_o_b, w_out,
           norm_ffn, w_router, b_router, w_gate, b_gate, w_up, b_up, w_down, b_down, norm_ple,
           w_ple_gate, w_ple_proj, norm_final):
    batch, seq, _ = x_prompt.shape
    n_seq, n_new, _ = x_sample.shape
    n_p = batch * seq
    n_s = n_seq * n_new
    n_pool = cache_k.shape[1]

    w = w_in[0]
    w_big = jnp.concatenate([w[:, 0:4096], w[:, 4112:7184], w[:, 7192:9240]], axis=1).astype(BF16)
    w_small = jnp.concatenate([w[:, 4096:4112], w[:, 7184:7192]], axis=1)
    w_small_t = jnp.pad(w_small.T, ((0, SMALL_ROWS - 24), (0, 0))).astype(BF16)
    gain_mix = norm_mix[0].reshape(1, D_MODEL)
    col = lambda v: v[0].reshape(N_HEADS, 1)
    w_oa_b, w_ob_b, w_out_b = (t[0].astype(BF16) for t in (w_o_a, w_o_b, w_out))
    wg_b, wu_b, wd_b = (t[0].astype(BF16) for t in (w_gate, w_up, w_down))
    bg, bu, bd = (t[0].reshape(N_EXPERTS, 1, D_MODEL) for t in (b_gate, b_up, b_down))
    w_pg_b = w_ple_gate[0].astype(BF16)
    w_pp_b = w_ple_proj[0].astype(BF16)

    w_router_p = jnp.pad(w_router[0], ((0, 0), (0, ROUTER_LANES - N_EXPERTS)))
    b_router_p = jnp.pad(b_router[0], (0, ROUTER_LANES - N_EXPERTS),
                         constant_values=-jnp.inf).reshape(1, ROUTER_LANES)

    cat = lambda a, b: jnp.concatenate([a, b], axis=0)
    x_all = cat(x_prompt.reshape(n_p, D_MODEL), x_sample.reshape(n_s, D_MODEL))
    pe_all = cat(p_prompt[0].reshape(n_p, -1), p_sample[0].reshape(n_s, -1))

    big, small_t = in_proj(x_all, gain_mix, w_big, w_small_t)
    small_p, small_s = small_t[:, :n_p], small_t[:, n_p:]

    lf_p, c_p = logf_scan(small_p, col(b_f), seq, seq)
    lf_s, c_s = logf_scan(small_s, col(b_f), n_new, n_s)

    def gdn_group(small_g, conv_state, s0, b, length, tok0):
        c = min(DN_CHUNK, length)
        small_c = small_g.reshape(SMALL_ROWS, b * length // c, c).transpose(1, 0, 2)
        conv_buf = jnp.pad(conv_state, ((0, 0), (8 - (CONV_W - 1), 0), (0, 0)))
        return gdn(big, small_c, conv_buf, s0, conv_w[0], col(a_log), col(dt_bias),
                   norm_o_a[0].reshape(1, HEAD_DIM), b, length, c, tok0)

    oa_p, s_p = gdn_group(small_p, jnp.zeros((batch, CONV_W - 1, CONV_CH), F32),
                          jnp.zeros((batch, N_HEADS, HEAD_DIM, HEAD_DIM), F32), batch, seq, 0)
    oa_s, s_s = gdn_group(small_s, state_conv[0], state_S[0], n_seq, n_new, n_p)

    ob_p = fox_prompt(big, c_p.reshape(N_HEADS, batch, 1, seq).transpose(1, 0, 2, 3), batch, seq)
    cn = c_s.reshape(N_HEADS, n_seq, n_new).transpose(1, 0, 2).reshape(n_seq, 1, N_HEADS * n_new)
    cache_f = cache_logf[0].reshape(n_pool, 1, PAGE_SIZE * N_HEADS)
    ob_s = fox_sample(big, cn, cache_k[0], cache_v[0], cache_f, page_table, n_seq, n_new, n_p)

    x1, h2, idx, wts, rank, cnt = merge(x_all, oa_p, oa_s, ob_p, ob_s, big, w_oa_b, w_ob_b, w_out_b,
                                        norm_ffn[0].reshape(1, D_MODEL), w_router_p, b_router_p)
    dest_flat, tile_expert, n_tiles, n_rows = _routing_tables(idx, rank, cnt)
    xs_sorted = dispatch(h2, dest_flat, n_rows)
    ys = moe(xs_sorted, tile_expert, n_tiles, wg_b, bg, wu_b, bu, wd_b, bd)
    y_all = combine(x1, wts, pe_all, dest_flat, ys, norm_ple[0].reshape(1, D_MODEL), w_pg_b, w_pp_b,
                    norm_final.reshape(1, D_MODEL))

    def states(lf_t, s_new, b, length, tok0):
        rows = lambda g: big[g, tok0:tok0 + b * length]
        head = lambda g: rows(g).reshape(1, b, length, N_HEADS, HEAD_DIM)
        last = lambda g: rows(g).reshape(b, length, D_MODEL)[:, length - (CONV_W - 1):, :]
        pre = jnp.concatenate([last(G_QA), last(G_KA), last(G_VA)], axis=2)
        return (head(G_KB), head(G_VB), lf_t.T.reshape(1, b, length, N_HEADS), s_new[None], pre[None])

    st_p = states(lf_p, s_p, batch, seq, 0)
    st_s = states(lf_s, s_s, n_seq, n_new, n_p)
    return (y_all[:n_p].reshape(batch, seq, D_MODEL), y_all[n_p:].reshape(n_seq, n_new, D_MODEL)) + st_p + st_s
```

```python
import functools

import jax
import jax.numpy as jnp
from jax import lax
from jax.experimental import pallas as pl
from jax.experimental.pallas import tpu as pltpu

F32 = jnp.float32
BF16 = jnp.bfloat16
I32 = jnp.int32

D_MODEL = 1024
N_HEADS = 8
HEAD_DIM = 128
CONV_W = 4
CONV_CH = 3 * D_MODEL
DN_CHUNK = 64
N_EXPERTS = 32
TOP_K = 4
PAGE_SIZE = 128
SWIGLU_LIMIT = 7.0
SWIGLU_ALPHA = 1.702
EPS = 1e-6
NEG_BIG = -1e30

G_QA, G_KA, G_VA, G_ZA, G_QB, G_KB, G_VB, G_GA, G_GB = range(9)
N_GROUPS = 9
SMALL_ROWS = 32

MIB = 1024 * 1024

ROUTER_LANES = 128

ROW_TILE = 512
MERGE_TILE = 256
MOE_TILE = 512
GATHER_TILE = 256
PAGES_PER_STEP = 16
SCAN_PAGES = 8
SOFTMAX_PAGES = 4
DMA_UNROLL = 2


def _params(sem, vmem_mib):
    return pltpu.CompilerParams(dimension_semantics=sem, vmem_limit_bytes=vmem_mib * MIB)


def _row_tile(n, cap):
    t = cap
    while n % t:
        t //= 2
    assert t >= 8, (n, cap)
    return t


def _bdot(a, b):
    return jnp.dot(a.astype(BF16), b.astype(BF16), preferred_element_type=F32)


def _bdot_nt(a, b):
    return lax.dot_general(a.astype(BF16), b.astype(BF16), (((1,), (1,)), ((), ())),
                           preferred_element_type=F32)


def _bdot_tn(a, b):
    return lax.dot_general(a.astype(BF16), b.astype(BF16), (((0,), (0,)), ((), ())),
                           preferred_element_type=F32)


def _split3(x):
    hi = x.astype(BF16)
    r1 = x - hi.astype(F32)
    mid = r1.astype(BF16)
    lo = (r1 - mid.astype(F32)).astype(BF16)
    return hi, mid, lo


def _sigmoid(x):
    return jax.nn.sigmoid(x)


def _silu(x):
    return x * jax.nn.sigmoid(x)


def _softplus(x):
    return jnp.maximum(x, 0.0) + jnp.log1p(jnp.exp(-jnp.abs(x)))


def _log_sigmoid(x):
    return jnp.minimum(x, 0.0) - jnp.log1p(jnp.exp(-jnp.abs(x)))


def _rms(x, g):
    return x * lax.rsqrt(jnp.mean(x * x, axis=-1, keepdims=True) + EPS) * g


def _in_proj_kernel(x_ref, g_ref, w_ref, ws_ref, big_ref, st_ref, h_ref):
    @pl.when(pl.program_id(1) == 0)
    def _():
        hb = _rms(x_ref[...], g_ref[...]).astype(BF16)
        h_ref[...] = hb
        st_ref[...] = lax.dot_general(ws_ref[...], hb, (((1,), (1,)), ((), ())),
                                      preferred_element_type=F32)

    big_ref[...] = jnp.dot(h_ref[...], w_ref[...], preferred_element_type=F32)


def in_proj(x, gain, w_big, w_small_t):
    n = x.shape[0]
    tm = _row_tile(n, ROW_TILE)
    return pl.pallas_call(
        _in_proj_kernel,
        grid=(n // tm, N_GROUPS),
        in_specs=[
            pl.BlockSpec((tm, D_MODEL), lambda i, g: (i, 0)),
            pl.BlockSpec((1, D_MODEL), lambda i, g: (0, 0)),
            pl.BlockSpec((D_MODEL, D_MODEL), lambda i, g: (0, g)),
            pl.BlockSpec((SMALL_ROWS, D_MODEL), lambda i, g: (0, 0)),
        ],
        out_specs=[
            pl.BlockSpec((None, tm, D_MODEL), lambda i, g: (g, i, 0)),
            pl.BlockSpec((SMALL_ROWS, tm), lambda i, g: (0, i)),
        ],
        out_shape=[
            jax.ShapeDtypeStruct((N_GROUPS, n, D_MODEL), F32),
            jax.ShapeDtypeStruct((SMALL_ROWS, n), F32),
        ],
        scratch_shapes=[pltpu.VMEM((tm, D_MODEL), BF16)],
        compiler_params=_params(("arbitrary", "arbitrary"),32),
        name="in_proj",
    )(x, gain, w_big, w_small_t)


def _logf_kernel(st_ref, bf_ref, lf_ref, c_ref, *, seg):
    lf = _log_sigmoid(st_ref[16:24, :] + bf_ref[...])
    lf_ref[...] = lf
    pos = lax.broadcasted_iota(I32, lf.shape, 1) % seg
    y = lf
    k = 1
    while k < seg:
        y = y + jnp.where(pos >= k, pltpu.roll(y, k, axis=1), 0.0)
        k *= 2
    c_ref[...] = y


def logf_scan(small_t, b_f, seg, block):
    n = small_t.shape[1]
    return pl.pallas_call(
        functools.partial(_logf_kernel, seg=seg),
        grid=(n // block,),
        in_specs=[
            pl.BlockSpec((SMALL_ROWS, block), lambda i: (0, i)),
            pl.BlockSpec((N_HEADS, 1), lambda i: (0, 0)),
        ],
        out_specs=[
            pl.BlockSpec((N_HEADS, block), lambda i: (0, i)),
            pl.BlockSpec((N_HEADS, block), lambda i: (0, i)),
        ],
        out_shape=[jax.ShapeDtypeStruct((N_HEADS, n), F32)] * 2,
        compiler_params=_params(("arbitrary",),16),
        name="logf_scan",
    )(small_t, b_f)


def _unit_lower_inverse(m, c, ii, jj, eye):
    def same_block(s):
        return (ii // s) == (jj // s)

    hs = range(len(m))
    d = [jnp.where(same_block(8), m[h], 0.0) for h in hs]
    d2 = [_bdot(d[h], d[h]) for h in hs]
    d4 = [_bdot(d2[h], d2[h]) for h in hs]
    x = [eye - d[h] for h in hs]
    x = [x[h] + _bdot(x[h], d2[h]) for h in hs]
    x = [x[h] + _bdot(x[h], d4[h]) for h in hs]
    s = 8
    while s < c:
        sel = same_block(2 * s) & jnp.logical_not(same_block(s))
        t = [_bdot(x[h], jnp.where(sel, m[h], 0.0)) for h in hs]
        x = [x[h] - _bdot(t[h], x[h]) for h in hs]
        s *= 2
    return x


def _gdn_kernel(q_ref, k_ref, v_ref, z_ref, sm_ref, cb_ref, s0_ref, cw_ref, al_ref, dt_ref, no_ref,
                o_ref, s_ref, full_ref, *, c):
    n = pl.program_id(1)

    @pl.when(n == 0)
    def _():
        full_ref[0:8, :] = cb_ref[...]
        s_ref[...] = s0_ref[...]

    full_ref[8:8 + c, 0:D_MODEL] = q_ref[...]
    full_ref[8:8 + c, D_MODEL:2 * D_MODEL] = k_ref[...]
    full_ref[8:8 + c, 2 * D_MODEL:3 * D_MODEL] = v_ref[...]

    conv = cw_ref[0:1, :] * full_ref[5:5 + c, :]
    for i in range(1, CONV_W):
        conv = conv + cw_ref[i:i + 1, :] * full_ref[5 + i:5 + i + c, :]
    u = _silu(conv)
    tail = full_ref[c:c + 8, :]
    full_ref[0:8, :] = tail

    sm = sm_ref[...]
    beta_all = _sigmoid(sm[0:8, :])
    g_all = -jnp.exp(al_ref[...]) * _softplus(sm[8:16, :] + dt_ref[...])

    ii = lax.broadcasted_iota(I32, (c, c), 0)
    jj = lax.broadcasted_iota(I32, (c, c), 1)
    incl = ii >= jj
    strict = ii > jj
    eye = (ii == jj).astype(F32)
    lower_b = incl.astype(BF16)
    upper_b = (ii <= jj).astype(BF16)
    eye_b = eye.astype(BF16)

    g_row = jnp.zeros((N_HEADS, c), F32)
    g_col = jnp.zeros((c, N_HEADS), F32)
    beta_col = jnp.zeros((c, N_HEADS), F32)
    for gp, bp in zip(_split3(g_all), _split3(beta_all)):
        g_row = g_row + jnp.dot(gp, upper_b, preferred_element_type=F32)
        g_col = g_col + lax.dot_general(lower_b, gp, (((1,), (1,)), ((), ())),
                                        preferred_element_type=F32)
        beta_col = beta_col + lax.dot_general(eye_b, bp, (((1,), (1,)), ((), ())),
                                              preferred_element_type=F32)

    hs = range(N_HEADS)
    sls = [slice(h * HEAD_DIM, (h + 1) * HEAD_DIM) for h in hs]
    qh, kh, vh = [], [], []
    for h in hs:
        q_ = u[:, h * HEAD_DIM:(h + 1) * HEAD_DIM]
        k_ = u[:, D_MODEL + h * HEAD_DIM:D_MODEL + (h + 1) * HEAD_DIM]
        qh.append(q_ * lax.rsqrt(jnp.sum(q_ * q_, axis=-1, keepdims=True) + EPS) * (HEAD_DIM ** -0.5))
        kh.append(k_ * lax.rsqrt(jnp.sum(k_ * k_, axis=-1, keepdims=True) + EPS))
        vh.append(u[:, 2 * D_MODEL + h * HEAD_DIM:2 * D_MODEL + (h + 1) * HEAD_DIM])
    gc = [g_col[:, h:h + 1] for h in hs]
    gr = [g_row[h:h + 1, :] for h in hs]
    bc = [beta_col[:, h:h + 1] for h in hs]
    decay = [jnp.where(incl, jnp.exp(jnp.where(incl, gc[h] - gr[h], 0.0)), 0.0) for h in hs]
    kb = [kh[h] * bc[h] for h in hs]
    kk = [_bdot_nt(kb[h], kh[h]) for h in hs]
    qk = [_bdot_nt(qh[h], kh[h]) * decay[h] for h in hs]
    m = [jnp.where(strict, kk[h] * decay[h], 0.0) for h in hs]
    eg = [jnp.exp(gc[h]) for h in hs]
    rhs = [jnp.concatenate([vh[h] * bc[h], kb[h] * eg[h]], axis=1) for h in hs]
    inv = _unit_lower_inverse(m, c, ii, jj, eye)
    sol = [_bdot(inv[h], rhs[h]) for h in hs]
    g_last = [gr[h][:, c - 1:c] for h in hs]
    q_dec = [qh[h] * eg[h] for h in hs]
    k_dec = [kh[h] * jnp.exp(g_last[h] - gc[h]) for h in hs]

    s_old = [s_ref[h] for h in hs]
    w_s = [_bdot(sol[h][:, HEAD_DIM:], s_old[h]) for h in hs]
    q_s = [_bdot(q_dec[h], s_old[h]) for h in hs]
    uu = [sol[h][:, :HEAD_DIM] - w_s[h] for h in hs]
    o = [q_s[h] + _bdot(qk[h], uu[h]) for h in hs]
    s_upd = [_bdot_tn(k_dec[h], uu[h]) for h in hs]
    for h in hs:
        s_ref[h] = s_old[h] * jnp.exp(g_last[h]) + s_upd[h]
        o_ref[:, sls[h]] = _rms(o[h], no_ref[...]) * _silu(z_ref[:, sls[h]])


def gdn(big, small_c, conv_buf, s0, conv_w, a_log, dt_bias, norm_o, batch, length, c, tok0):
    n_tok = batch * length
    nc = length // c
    blk0 = tok0 // c

    def grp(g):
        return pl.BlockSpec((None, c, D_MODEL), lambda b, n, g=g: (g, blk0 + b * nc + n, 0))

    def full(shape):
        return pl.BlockSpec(shape, lambda b, n: (0,) * len(shape))

    return pl.pallas_call(
        functools.partial(_gdn_kernel, c=c),
        grid=(batch, nc),
        in_specs=[
            grp(G_QA), grp(G_KA), grp(G_VA), grp(G_ZA),
            pl.BlockSpec((None, SMALL_ROWS, c), lambda b, n: (b * nc + n, 0, 0)),
            pl.BlockSpec((None, 8, CONV_CH), lambda b, n: (b, 0, 0)),
            pl.BlockSpec((None, N_HEADS, HEAD_DIM, HEAD_DIM), lambda b, n: (b, 0, 0, 0)),
            full((CONV_W, CONV_CH)), full((N_HEADS, 1)), full((N_HEADS, 1)), full((1, HEAD_DIM)),
        ],
        out_specs=[
            pl.BlockSpec((c, D_MODEL), lambda b, n: (b * nc + n, 0)),
            pl.BlockSpec((None, N_HEADS, HEAD_DIM, HEAD_DIM), lambda b, n: (b, 0, 0, 0)),
        ],
        out_shape=[
            jax.ShapeDtypeStruct((n_tok, D_MODEL), F32),
            jax.ShapeDtypeStruct((batch, N_HEADS, HEAD_DIM, HEAD_DIM), F32),
        ],
        scratch_shapes=[pltpu.VMEM((c + 8, CONV_CH), F32)],
        compiler_params=_params(("arbitrary", "arbitrary"),32),
        name="gdn",
    )(big, big, big, big, small_c, conv_buf, s0, conv_w, a_log, dt_bias, norm_o)


def _fox_prompt_kernel(q_ref, k_ref, v_ref, c_ref, o_ref, *, tq, tk):
    qi = pl.program_id(2)
    q0 = pl.multiple_of(qi * tq, tq)
    c_base = jnp.max(c_ref[:, pl.ds(q0, 128)], axis=1, keepdims=True)
    n_parts = 2 if tq % 16 == 0 else 1
    rp = tq // n_parts
    parts = [slice(i * rp, (i + 1) * rp) for i in range(n_parts)]
    q_parts = [(q_ref[sl, :] * (HEAD_DIM ** -0.5)).astype(BF16) for sl in parts]
    rows = q0 + lax.broadcasted_iota(I32, (rp, tk), 0)
    cols = lax.broadcasted_iota(I32, (rp, tk), 1)

    def block(ki, carry, masked):
        k0 = pl.multiple_of(ki * tk, tk)
        kb = k_ref[pl.ds(k0, tk), :].astype(BF16)
        vb = v_ref[pl.ds(k0, tk), :].astype(BF16)
        bias = c_base - c_ref[:, pl.ds(k0, tk)]
        ss = [_bdot_nt(qp, kb) + bias for qp in q_parts]
        if masked:
            ss = [jnp.where(cols + k0 <= rows + i * rp, s, NEG_BIG) for i, s in enumerate(ss)]
        ms, ls, ps = [], [], []
        for (m, l, _), s in zip(carry, ss):
            m_new = jnp.maximum(m, jnp.max(s, axis=1, keepdims=True))
            alpha = jnp.exp(m - m_new)
            p = jnp.exp(s - m_new)
            ms.append((m_new, alpha))
            ls.append(l * alpha + jnp.sum(p, axis=1, keepdims=True))
            ps.append(p)
        pvs = [_bdot(p, vb) for p in ps]
        return tuple((m_new, l, acc * alpha + pv)
                     for (m_new, alpha), l, (_, _, acc), pv in zip(ms, ls, carry, pvs))

    init = tuple((jnp.full((rp, 1), NEG_BIG, F32), jnp.zeros((rp, 1), F32),
                  jnp.zeros((rp, HEAD_DIM), F32)) for _ in parts)
    carry = lax.fori_loop(0, qi, lambda ki, c: block(ki, c, False), init)
    carry = block(qi, carry, True)
    o_ref[...] = jnp.concatenate([acc / l for (_, l, acc) in carry], axis=0)


def fox_prompt(big, c_t, batch, seq):
    tq = min(512, seq)
    tk = tq
    nq = seq // tq

    def grp(g):
        return pl.BlockSpec((None, seq, HEAD_DIM), lambda b, h, qi, g=g: (g, b, h))

    return pl.pallas_call(
        functools.partial(_fox_prompt_kernel, tq=tq, tk=tk),
        grid=(batch, N_HEADS, nq),
        in_specs=[
            pl.BlockSpec((None, tq, HEAD_DIM), lambda b, h, qi: (G_QB, b * nq + qi, h)),
            grp(G_KB), grp(G_VB),
            pl.BlockSpec((None, None, 1, seq), lambda b, h, qi: (b, h, 0, 0)),
        ],
        out_specs=pl.BlockSpec((tq, HEAD_DIM), lambda b, h, qi: (b * nq + qi, h)),
        out_shape=jax.ShapeDtypeStruct((batch * seq, D_MODEL), F32),
        compiler_params=_params(("arbitrary", "arbitrary", "arbitrary"), 32),
        name="fox_prompt",
    )(big, big, big, c_t)


def _fox_sample_kernel(pt_ref, *refs, n_new, pps):
    del pt_ref
    k_pages = refs[0:pps]
    v_pages = refs[pps:2 * pps]
    f_pages = refs[2 * pps:3 * pps]
    q_ref, kn_ref, vn_ref, cn_ref = refs[3 * pps:3 * pps + 4]
    o_ref = refs[3 * pps + 4]
    qf_ref, m_ref, l_ref, acc_ref, carry_ref = refs[3 * pps + 5:]
    step = pl.program_id(1)
    n_rows = N_HEADS * n_new
    page_cols = PAGE_SIZE * N_HEADS

    def by_head(x):
        return jnp.concatenate([x[:, h * HEAD_DIM:(h + 1) * HEAD_DIM] for h in range(N_HEADS)], axis=0)

    @pl.when(step == 0)
    def _():
        qf = by_head(q_ref[...] * (HEAD_DIM ** -0.5)).astype(BF16)
        qf_ref[...] = qf
        s = _bdot_nt(qf, by_head(kn_ref[...])) - cn_ref[...]
        r_i = lax.broadcasted_iota(I32, (n_rows, n_rows), 0)
        c_i = lax.broadcasted_iota(I32, (n_rows, n_rows), 1)
        ok = (r_i // n_new == c_i // n_new) & (c_i % n_new <= r_i % n_new)
        s = jnp.where(ok, s, NEG_BIG)
        m = jnp.max(s, axis=1, keepdims=True)
        p = jnp.exp(s - m)
        m_ref[...] = m
        l_ref[...] = jnp.sum(p, axis=1, keepdims=True)
        acc_ref[...] = _bdot(p, by_head(vn_ref[...]))
        carry_ref[...] = jnp.zeros_like(carry_ref)

    r_h = lax.broadcasted_iota(I32, (n_rows, page_cols), 0) // n_new
    c_h = lax.broadcasted_iota(I32, (n_rows, page_cols), 1) % N_HEADS
    same_head = r_h == c_h
    qf = qf_ref[...]
    carry = carry_ref[...]

    bias = [None] * pps
    for g0 in range(pps - SCAN_PAGES, -1, -SCAN_PAGES):
        lf = jnp.concatenate([f_pages[g0 + i][...] for i in range(SCAN_PAGES)], axis=0)
        lane = lax.broadcasted_iota(I32, lf.shape, 1)
        row = lax.broadcasted_iota(I32, lf.shape, 0)
        y = lf
        tot = lf
        k = N_HEADS
        while k < page_cols:
            y = y + jnp.where(lane + k < page_cols, pltpu.roll(y, page_cols - k, axis=1), 0.0)
            tot = tot + pltpu.roll(tot, page_cols - k, axis=1)
            k *= 2
        later = tot
        k = 1
        while k < SCAN_PAGES:
            later = later + jnp.where(row + k < SCAN_PAGES, pltpu.roll(later, SCAN_PAGES - k, axis=0), 0.0)
            k *= 2
        r = (y - lf) + (later - tot) + carry
        carry = carry + later[0:1, :]
        for i in range(SCAN_PAGES):
            bias[g0 + i] = r[i:i + 1, :]
    carry_ref[...] = carry

    def group_scores(g):
        cols = []
        for p_i in range(g * SOFTMAX_PAGES, (g + 1) * SOFTMAX_PAGES):
            k2 = k_pages[p_i][...].reshape(page_cols, HEAD_DIM)
            cols.append(jnp.where(same_head, _bdot_nt(qf, k2) + bias[p_i], NEG_BIG))
        return jnp.concatenate(cols, axis=1)

    m, l, acc = m_ref[...], l_ref[...], acc_ref[...]
    n_groups = pps // SOFTMAX_PAGES
    s_next = group_scores(0)
    for g in range(n_groups):
        s = s_next
        if g + 1 < n_groups:
            s_next = group_scores(g + 1)
        m_new = jnp.maximum(m, jnp.max(s, axis=1, keepdims=True))
        alpha = jnp.exp(m - m_new)
        p = jnp.exp(s - m_new)
        l = l * alpha + jnp.sum(p, axis=1, keepdims=True)
        pv = jnp.zeros((n_rows, HEAD_DIM), F32)
        for i in range(SOFTMAX_PAGES):
            v2 = v_pages[g * SOFTMAX_PAGES + i][...].reshape(page_cols, HEAD_DIM)
            pv = pv + _bdot(p[:, i * page_cols:(i + 1) * page_cols], v2)
        acc = acc * alpha + pv
        m = m_new
    m_ref[...] = m
    l_ref[...] = l
    acc_ref[...] = acc

    @pl.when(step == pl.num_programs(1) - 1)
    def _():
        out = acc_ref[...] / l_ref[...]
        o_ref[...] = jnp.concatenate([out[h * n_new:(h + 1) * n_new, :] for h in range(N_HEADS)], axis=1)


def fox_sample(big, cn, cache_k, cache_v, cache_f, page_table, n_seq, n_new, tok0):
    n_pages = page_table.shape[1]
    pps = min(PAGES_PER_STEP, n_pages)
    assert n_pages % pps == 0 and pps % SCAN_PAGES == 0, (n_pages, pps)
    n_steps = n_pages // pps
    n_rows = N_HEADS * n_new
    page_cols = PAGE_SIZE * N_HEADS
    blk0 = tok0 // n_new

    def page_spec(shape, p_i):
        return pl.BlockSpec(shape, lambda b, j, pt, p_i=p_i: (
            (pt[b * n_pages + (n_steps - 1 - j) * pps + p_i],) + (0,) * (len(shape) - 1)))

    def grp(g):
        return pl.BlockSpec((None, n_new, D_MODEL), lambda b, j, pt, g=g: (g, blk0 + b, 0))

    kv_shape = (None, PAGE_SIZE, N_HEADS, HEAD_DIM)
    in_specs = ([page_spec(kv_shape, p_i) for p_i in range(pps)]
                + [page_spec(kv_shape, p_i) for p_i in range(pps)]
                + [page_spec((None, 1, page_cols), p_i) for p_i in range(pps)]
                + [grp(G_QB), grp(G_KB), grp(G_VB),
                   pl.BlockSpec((None, 1, n_rows), lambda b, j, pt: (b, 0, 0))])
    grid_spec = pltpu.PrefetchScalarGridSpec(
        num_scalar_prefetch=1,
        grid=(n_seq, n_steps),
        in_specs=in_specs,
        out_specs=pl.BlockSpec((n_new, D_MODEL), lambda b, j, pt: (b, 0)),
        scratch_shapes=[
            pltpu.VMEM((n_rows, HEAD_DIM), BF16),
            pltpu.VMEM((n_rows, 1), F32),
            pltpu.VMEM((n_rows, 1), F32),
            pltpu.VMEM((n_rows, HEAD_DIM), F32),
            pltpu.VMEM((1, page_cols), F32),
        ],
    )
    args = [cache_k] * pps + [cache_v] * pps + [cache_f] * pps + [big, big, big, cn]
    return pl.pallas_call(
        functools.partial(_fox_sample_kernel, n_new=n_new, pps=pps),
        grid_spec=grid_spec,
        out_shape=jax.ShapeDtypeStruct((n_seq * n_new, D_MODEL), F32),
        compiler_params=_params(("arbitrary", "arbitrary"), 48),
        name="fox_sample",
    )(page_table.reshape(-1), *args)


def _merge_kernel(x_ref, oap_ref, oas_ref, obp_ref, obs_ref, ga_ref, gb_ref, woa_ref, wob_ref,
                  wout_ref, nf_ref, wr_ref, br_ref, x1_ref, h2_ref, idx_ref, wts_ref, rank_ref,
                  cnt_ref, *, prompt_tiles):
    is_prompt = pl.program_id(0) < prompt_tiles
    oa = jnp.where(is_prompt, oap_ref[...], oas_ref[...])
    ob = jnp.where(is_prompt, obp_ref[...], obs_ref[...])
    ya = _bdot(oa, woa_ref[...])
    yb = _bdot(ob, wob_ref[...])
    mixed = _sigmoid(ga_ref[...]) * ya + _sigmoid(gb_ref[...]) * yb
    x1 = x_ref[...] + _bdot(mixed, wout_ref[...])
    x1_ref[...] = x1
    h2 = _rms(x1, nf_ref[...])
    h2_ref[...] = h2

    logits = jnp.dot(h2, wr_ref[...], preferred_element_type=F32,
                     precision=lax.Precision.HIGHEST) + br_ref[...]
    tm = logits.shape[0]
    lane = lax.broadcasted_iota(I32, logits.shape, 1).astype(F32)
    vals, idxs = [], []
    work = logits
    for _ in range(TOP_K):
        v = jnp.max(work, axis=1, keepdims=True)
        i = jnp.min(jnp.where(work == v, lane, float(ROUTER_LANES)), axis=1, keepdims=True)
        vals.append(v)
        idxs.append(i)
        work = jnp.where(lane == i, -jnp.inf, work)
    es = [jnp.exp(v - vals[0]) for v in vals]
    denom = es[0] + es[1] + es[2] + es[3]

    onehot = jnp.zeros(logits.shape, F32)
    for i in idxs:
        onehot = onehot + (lane == i).astype(F32)
    ti = lax.broadcasted_iota(I32, (tm, tm), 0)
    tj = lax.broadcasted_iota(I32, (tm, tm), 1)
    before = jnp.dot((tj < ti).astype(BF16), onehot.astype(BF16), preferred_element_type=F32)
    cnt_ref[...] = jnp.sum(onehot, axis=0, keepdims=True).astype(I32)

    l4 = lax.broadcasted_iota(I32, (tm, TOP_K), 1)
    idx_out = jnp.zeros((tm, TOP_K), I32)
    wts_out = jnp.zeros((tm, TOP_K), F32)
    rank_out = jnp.zeros((tm, TOP_K), I32)
    for k in range(TOP_K):
        rk = jnp.sum(jnp.where(lane == idxs[k], before, 0.0), axis=1, keepdims=True)
        idx_out = jnp.where(l4 == k, idxs[k].astype(I32), idx_out)
        wts_out = jnp.where(l4 == k, es[k] / denom, wts_out)
        rank_out = jnp.where(l4 == k, rk.astype(I32), rank_out)
    idx_ref[...] = idx_out
    wts_ref[...] = wts_out
    rank_ref[...] = rank_out


def merge(x, oa_p, oa_s, ob_p, ob_s, big, w_oa, w_ob, w_out, norm_ffn, w_router, b_router):
    n = x.shape[0]
    tm = _row_tile(oa_s.shape[0], _row_tile(oa_p.shape[0], MERGE_TILE))
    nt = n // tm
    ntp = oa_p.shape[0] // tm
    nts = oa_s.shape[0] // tm
    row = pl.BlockSpec((tm, D_MODEL), lambda i: (i, 0))
    row_p = pl.BlockSpec((tm, D_MODEL), lambda i: (jnp.minimum(i, ntp - 1), 0))
    row_s = pl.BlockSpec((tm, D_MODEL), lambda i: (jnp.clip(i - ntp, 0, nts - 1), 0))
    wfull = pl.BlockSpec((D_MODEL, D_MODEL), lambda i: (0, 0))
    k4 = pl.BlockSpec((tm, TOP_K), lambda i: (i, 0))
    return pl.pallas_call(
        functools.partial(_merge_kernel, prompt_tiles=ntp),
        grid=(nt,),
        in_specs=[
            row, row_p, row_s, row_p, row_s,
            pl.BlockSpec((None, tm, D_MODEL), lambda i: (G_GA, i, 0)),
            pl.BlockSpec((None, tm, D_MODEL), lambda i: (G_GB, i, 0)),
            wfull, wfull, wfull,
            pl.BlockSpec((1, D_MODEL), lambda i: (0, 0)),
            pl.BlockSpec((D_MODEL, ROUTER_LANES), lambda i: (0, 0)),
            pl.BlockSpec((1, ROUTER_LANES), lambda i: (0, 0)),
        ],
        out_specs=[row, row, k4, k4, k4, pl.BlockSpec((None, 1, ROUTER_LANES), lambda i: (i, 0, 0))],
        out_shape=[
            jax.ShapeDtypeStruct((n, D_MODEL), F32),
            jax.ShapeDtypeStruct((n, D_MODEL), F32),
            jax.ShapeDtypeStruct((n, TOP_K), I32),
            jax.ShapeDtypeStruct((n, TOP_K), F32),
            jax.ShapeDtypeStruct((n, TOP_K), I32),
            jax.ShapeDtypeStruct((nt, 1, ROUTER_LANES), I32),
        ],
        compiler_params=_params(("arbitrary",),56),
        name="merge",
    )(x, oa_p, oa_s, ob_p, ob_s, big, big, w_oa, w_ob, w_out, norm_ffn, w_router, b_router)


def _row_copy(src_ref, src_row, dst_ref, dst_row, sem):
    return pltpu.make_async_copy(src_ref.at[pl.ds(src_row, 1), :], dst_ref.at[pl.ds(dst_row, 1), :], sem)


def _dispatch_kernel(h_ref, dest_ref, xs_in_ref, xs_ref, sem):
    del xs_in_ref
    n_copies = dest_ref.shape[0]

    tm = h_ref.shape[0]
    assert n_copies == tm * TOP_K

    def start(t, carry):
        for k in range(TOP_K):
            _row_copy(h_ref, t, xs_ref, dest_ref[t * TOP_K + k], sem).start()
        return carry

    lax.fori_loop(0, tm, start, 0, unroll=DMA_UNROLL)
    for _ in range(TOP_K):
        pltpu.make_async_copy(h_ref, xs_ref.at[pl.ds(0, tm), :], sem).wait()


def dispatch(h2, dest_flat, n_rows):
    n = h2.shape[0]
    tm = _row_tile(n, GATHER_TILE)
    xs_init = jnp.zeros((n_rows, D_MODEL), F32)
    return pl.pallas_call(
        _dispatch_kernel,
        grid=(n // tm,),
        in_specs=[
            pl.BlockSpec((tm, D_MODEL), lambda i: (i, 0)),
            pl.BlockSpec((tm * TOP_K,), lambda i: (i,), memory_space=pltpu.SMEM),
            pl.BlockSpec(memory_space=pl.ANY),
        ],
        out_specs=pl.BlockSpec(memory_space=pl.ANY),
        out_shape=jax.ShapeDtypeStruct((n_rows, D_MODEL), F32),
        scratch_shapes=[pltpu.SemaphoreType.DMA(())],
        input_output_aliases={2: 0},
        compiler_params=_params(("arbitrary",), 16),
        name="dispatch",
    )(h2, dest_flat, xs_init)


def _moe_kernel(te_ref, nt_ref, x_ref, wg_ref, bg_ref, wu_ref, bu_ref, wd_ref, bd_ref, y_ref,
                wgb_ref, wub_ref, wdb_ref):
    g = pl.program_id(0)
    active = g < nt_ref[0]
    new_expert = jnp.logical_or(g == 0, te_ref[g] != te_ref[jnp.maximum(g - 1, 0)])

    @pl.when(jnp.logical_and(active, new_expert))
    def _():
        wgb_ref[...] = wg_ref[...].astype(BF16)
        wub_ref[...] = wu_ref[...].astype(BF16)
        wdb_ref[...] = wd_ref[...].astype(BF16)

    @pl.when(active)
    def _():
        x = x_ref[...].astype(BF16)
        gt = jnp.minimum(jnp.dot(x, wgb_ref[...], preferred_element_type=F32) + bg_ref[...], SWIGLU_LIMIT)
        up = jnp.clip(jnp.dot(x, wub_ref[...], preferred_element_type=F32) + bu_ref[...],
                      -SWIGLU_LIMIT, SWIGLU_LIMIT)
        act = (up + 1.0) * gt * _sigmoid(SWIGLU_ALPHA * gt)
        y_ref[...] = _bdot(act, wdb_ref[...]) + bd_ref[...]

    @pl.when(g >= nt_ref[0])
    def _():
        y_ref[...] = jnp.zeros_like(y_ref)


def moe(xs, tile_expert, n_tiles, w_gate, b_gate, w_up, b_up, w_down, b_down):
    n_rows = xs.shape[0]
    tm = MOE_TILE
    wspec = pl.BlockSpec((None, D_MODEL, D_MODEL), lambda g, te, nt: (te[g], 0, 0))
    bspec = pl.BlockSpec((None, 1, D_MODEL), lambda g, te, nt: (te[g], 0, 0))
    row = pl.BlockSpec((tm, D_MODEL), lambda g, te, nt: (g, 0))
    grid_spec = pltpu.PrefetchScalarGridSpec(
        num_scalar_prefetch=2,
        grid=(n_rows // tm,),
        in_specs=[row, wspec, bspec, wspec, bspec, wspec, bspec],
        out_specs=row,
        scratch_shapes=[pltpu.VMEM((D_MODEL, D_MODEL), BF16)] * 3,
    )
    return pl.pallas_call(
        _moe_kernel,
        grid_spec=grid_spec,
        out_shape=jax.ShapeDtypeStruct((n_rows, D_MODEL), F32),
        compiler_params=_params(("arbitrary",), 56),
        name="moe",
    )(tile_expert, n_tiles, xs, w_gate, b_gate, w_up, b_up, w_down, b_down)


def _combine_kernel(x1_ref, wts_ref, pe_ref, dest_ref, ys_ref, np_ref, wpg_ref, wpp_ref, nfin_ref,
                    y_ref, buf_ref, sem):
    tm = x1_ref.shape[0]
    n_copies = tm * TOP_K

    def start(t, carry):
        for k in range(TOP_K):
            _row_copy(ys_ref, dest_ref[t * TOP_K + k], buf_ref.at[k], t, sem).start()
        return carry

    lax.fori_loop(0, tm, start, 0, unroll=DMA_UNROLL)
    pe_proj = _bdot(pe_ref[...], wpp_ref[...])
    for k in range(TOP_K):
        pltpu.make_async_copy(ys_ref.at[pl.ds(0, tm), :], buf_ref.at[k], sem).wait()

    w = wts_ref[...]
    x2 = x1_ref[...]
    for k in range(TOP_K):
        x2 = x2 + w[:, k:k + 1] * buf_ref[k]
    gate = _sigmoid(_bdot(_rms(x2, np_ref[...]), wpg_ref[...]))
    x3 = x2 + pe_proj * gate
    y_ref[...] = _rms(x3, nfin_ref[...])


def combine(x1, wts, pe, dest_flat, ys, norm_ple, w_ple_gate, w_ple_proj, norm_final):
    n = x1.shape[0]
    tm = _row_tile(n, GATHER_TILE)
    ple = pe.shape[1]
    row = pl.BlockSpec((tm, D_MODEL), lambda i: (i, 0))
    vec = pl.BlockSpec((1, D_MODEL), lambda i: (0, 0))
    return pl.pallas_call(
        _combine_kernel,
        grid=(n // tm,),
        in_specs=[
            row,
            pl.BlockSpec((tm, TOP_K), lambda i: (i, 0)),
            pl.BlockSpec((tm, ple), lambda i: (i, 0)),
            pl.BlockSpec((tm * TOP_K,), lambda i: (i,), memory_space=pltpu.SMEM),
            pl.BlockSpec(memory_space=pl.ANY),
            vec,
            pl.BlockSpec((D_MODEL, D_MODEL), lambda i: (0, 0)),
            pl.BlockSpec((ple, D_MODEL), lambda i: (0, 0)),
            vec,
        ],
        out_specs=row,
        out_shape=jax.ShapeDtypeStruct((n, D_MODEL), F32),
        scratch_shapes=[pltpu.VMEM((TOP_K, tm, D_MODEL), F32), pltpu.SemaphoreType.DMA(())],
        compiler_params=_params(("arbitrary",), 32),
        name="combine",
    )(x1, wts, pe, dest_flat, ys, norm_ple, w_ple_gate, w_ple_proj, norm_final)


def _routing_tables(idx, rank, counts):
    n = idx.shape[0]
    tile = n // counts.shape[0]
    cnt = counts[:, 0, :N_EXPERTS]
    base = jnp.cumsum(cnt, axis=0) - cnt
    total = jnp.sum(cnt, axis=0)
    tiles_per_e = (total + MOE_TILE - 1) // MOE_TILE
    tile_end = jnp.cumsum(tiles_per_e)
    start_row = (tile_end - tiles_per_e) * MOE_TILE
    offs = (start_row[None, :] + base).astype(I32)
    offs_tok = jnp.broadcast_to(offs[:, None, None, :], (n // tile, tile, 1, N_EXPERTS))
    onehot = idx.reshape(n // tile, tile, TOP_K, 1) == jnp.arange(N_EXPERTS, dtype=I32)
    dest = jnp.sum(jnp.where(onehot, offs_tok, 0), axis=-1).reshape(n, TOP_K) + rank
    n_rows = (pl.cdiv(n * TOP_K, MOE_TILE) + N_EXPERTS) * MOE_TILE
    g = jnp.arange(n_rows // MOE_TILE, dtype=I32)
    n_tiles = tile_end[-1].astype(I32)
    g_eff = jnp.minimum(g, n_tiles - 1)
    tile_expert = jnp.sum((g_eff[:, None] >= tile_end[None, :]).astype(I32), axis=1)
    return dest.reshape(-1).astype(I32), tile_expert, n_tiles.reshape(1), n_rows


def kernel(x_prompt, x_sample, p_prompt, p_sample, cache_k, cache_v, cache_logf, state_S, state_conv,
           page_table, norm_mix, w_in, conv_w, a_log, dt_bias, norm_o_a, w_o_a, b_f, w_o_b, w_out,
           norm_ffn, w_router, b_router, w_gate, b_gate, w_up, b_up, w_down, b_down, norm_ple,
           w_ple_gate, w_ple_proj, norm_final):
    batch, seq, _ = x_prompt.shape
    n_seq, n_new, _ = x_sample.shape
    n_p = batch * seq
    n_s = n_seq * n_new
    n_pool = cache_k.shape[1]

    w = w_in[0]
    w_big = jnp.concatenate([w[:, 0:4096], w[:, 4112:7184], w[:, 7192:9240]], axis=1).astype(BF16)
    w_small = jnp.concatenate([w[:, 4096:4112], w[:, 7184:7192]], axis=1)
    w_small_t = jnp.pad(w_small.T, ((0, SMALL_ROWS - 24), (0, 0))).astype(BF16)
    gain_mix = norm_mix[0].reshape(1, D_MODEL)
    col = lambda v: v[0].reshape(N_HEADS, 1)
    w_oa_b, w_ob_b, w_out_b = (t[0].astype(BF16) for t in (w_o_a, w_o_b, w_out))
    wg_b, wu_b, wd_b = w_gate[0], w_up[0], w_down[0]
    bg, bu, bd = (t[0].reshape(N_EXPERTS, 1, D_MODEL) for t in (b_gate, b_up, b_down))
    w_pg_b = w_ple_gate[0].astype(BF16)
    w_pp_b = w_ple_proj[0].astype(BF16)

    w_router_p = jnp.pad(w_router[0], ((0, 0), (0, ROUTER_LANES - N_EXPERTS)))
    b_router_p = jnp.pad(b_router[0], (0, ROUTER_LANES - N_EXPERTS),
                         constant_values=-jnp.inf).reshape(1, ROUTER_LANES)

    cat = lambda a, b: jnp.concatenate([a, b], axis=0)
    x_all = cat(x_prompt.reshape(n_p, D_MODEL), x_sample.reshape(n_s, D_MODEL))
    pe_all = cat(p_prompt[0].reshape(n_p, -1), p_sample[0].reshape(n_s, -1))

    big, small_t = in_proj(x_all, gain_mix, w_big, w_small_t)
    small_p, small_s = small_t[:, :n_p], small_t[:, n_p:]

    lf_p, c_p = logf_scan(small_p, col(b_f), seq, seq)
    lf_s, c_s = logf_scan(small_s, col(b_f), n_new, n_s)

    def gdn_group(small_g, conv_state, s0, b, length, tok0):
        c = min(DN_CHUNK, length)
        small_c = small_g.reshape(SMALL_ROWS, b * length // c, c).transpose(1, 0, 2)
        conv_buf = jnp.pad(conv_state, ((0, 0), (8 - (CONV_W - 1), 0), (0, 0)))
        return gdn(big, small_c, conv_buf, s0, conv_w[0], col(a_log), col(dt_bias),
                   norm_o_a[0].reshape(1, HEAD_DIM), b, length, c, tok0)

    oa_p, s_p = gdn_group(small_p, jnp.zeros((batch, CONV_W - 1, CONV_CH), F32),
                          jnp.zeros((batch, N_HEADS, HEAD_DIM, HEAD_DIM), F32), batch, seq, 0)
    oa_s, s_s = gdn_group(small_s, state_conv[0], state_S[0], n_seq, n_new, n_p)

    ob_p = fox_prompt(big, c_p.reshape(N_HEADS, batch, 1, seq).transpose(1, 0, 2, 3), batch, seq)
    cn = c_s.reshape(N_HEADS, n_seq, n_new).transpose(1, 0, 2).reshape(n_seq, 1, N_HEADS * n_new)
    cache_f = cache_logf[0].reshape(n_pool, 1, PAGE_SIZE * N_HEADS)
    ob_s = fox_sample(big, cn, cache_k[0], cache_v[0], cache_f, page_table, n_seq, n_new, n_p)

    x1, h2, idx, wts, rank, cnt = merge(x_all, oa_p, oa_s, ob_p, ob_s, big, w_oa_b, w_ob_b, w_out_b,
                                        norm_ffn[0].reshape(1, D_MODEL), w_router_p, b_router_p)
    dest_flat, tile_expert, n_tiles, n_rows = _routing_tables(idx, rank, cnt)
    xs_sorted = dispatch(h2, dest_flat, n_rows)
    ys = moe(xs_sorted, tile_expert, n_tiles, wg_b, bg, wu_b, bu, wd_b, bd)
    y_all = combine(x1, wts, pe_all, dest_flat, ys, norm_ple[0].reshape(1, D_MODEL), w_pg_b, w_pp_b,
                    norm_final.reshape(1, D_MODEL))

    def states(lf_t, s_new, b, length, tok0):
        rows = lambda g: big[g, tok0:tok0 + b * length]
        head = lambda g: rows(g).reshape(1, b, length, N_HEADS, HEAD_DIM)
        last = lambda g: rows(g).reshape(b, length, D_MODEL)[:, length - (CONV_W - 1):, :]
        pre = jnp.concatenate([last(G_QA), last(G_KA), last(G_VA)], axis=2)
        return (head(G_KB), head(G_VB), lf_t.T.reshape(1, b, length, N_HEADS), s_new[None], pre[None])

    st_p = states(lf_p, s_p, batch, seq, 0)
    st_s = states(lf_s, s_s, n_seq, n_new, n_p)
    return (y_all[:n_p].reshape(batch, seq, D_MODEL), y_all[n_p:].reshape(n_seq, n_new, D_MODEL)) + st_p + st_s
```

```python
import functools

import jax
import jax.numpy as jnp
from jax import lax
from jax.experimental import pallas as pl
from jax.experimental.pallas import tpu as pltpu

F32 = jnp.float32
BF16 = jnp.bfloat16
I32 = jnp.int32

D_MODEL = 1024
N_HEADS = 8
HEAD_DIM = 128
CONV_W = 4
CONV_CH = 3 * D_MODEL
DN_CHUNK = 64
N_EXPERTS = 32
TOP_K = 4
PAGE_SIZE = 128
SWIGLU_LIMIT = 7.0
SWIGLU_ALPHA = 1.702
EPS = 1e-6
NEG_BIG = -1e30
LOG2_E = 1.4426950408889634

G_QA, G_KA, G_VA, G_ZA, G_QB, G_KB, G_VB, G_GA, G_GB = range(9)
N_GROUPS = 9
SMALL_ROWS = 32

MIB = 1024 * 1024

ROUTER_LANES = 128

GDN_CHUNKS_PER_STEP = 4
ROW_TILE = 1024
MERGE_TILE = 256
MOE_TILE = 512
GATHER_TILE = 256
PAGES_PER_STEP = 16
SCAN_PAGES = 8
SOFTMAX_PAGES = 4
DMA_UNROLL = 2


def _params(sem, vmem_mib):
    return pltpu.CompilerParams(dimension_semantics=sem, vmem_limit_bytes=vmem_mib * MIB)


def _row_tile(n, cap):
    t = cap
    while n % t:
        t //= 2
    assert t >= 8, (n, cap)
    return t


def _bdot(a, b):
    return jnp.dot(a.astype(BF16), b.astype(BF16), preferred_element_type=F32)


def _bdot_nt(a, b):
    return lax.dot_general(a.astype(BF16), b.astype(BF16), (((1,), (1,)), ((), ())),
                           preferred_element_type=F32)


def _bdot_tn(a, b):
    return lax.dot_general(a.astype(BF16), b.astype(BF16), (((0,), (0,)), ((), ())),
                           preferred_element_type=F32)


def _split3(x):
    hi = x.astype(BF16)
    r1 = x - hi.astype(F32)
    mid = r1.astype(BF16)
    lo = (r1 - mid.astype(F32)).astype(BF16)
    return hi, mid, lo


def _sigmoid(x):
    return jax.nn.sigmoid(x)


def _silu(x):
    return x * jax.nn.sigmoid(x)


def _softplus(x):
    return jnp.maximum(x, 0.0) + jnp.log1p(jnp.exp(-jnp.abs(x)))


def _log_sigmoid(x):
    return jnp.minimum(x, 0.0) - jnp.log1p(jnp.exp(-jnp.abs(x)))


def _rms(x, g):
    return x * lax.rsqrt(jnp.mean(x * x, axis=-1, keepdims=True) + EPS) * g


def _in_proj_kernel(x_ref, g_ref, w_ref, ws_ref, big_ref, st_ref, h_ref):
    @pl.when(pl.program_id(1) == 0)
    def _():
        hb = _rms(x_ref[...], g_ref[...]).astype(BF16)
        h_ref[...] = hb
        st_ref[...] = lax.dot_general(ws_ref[...], hb, (((1,), (1,)), ((), ())),
                                      preferred_element_type=F32)

    big_ref[...] = jnp.dot(h_ref[...], w_ref[...], preferred_element_type=F32)


def in_proj(x, gain, w_big, w_small_t):
    n = x.shape[0]
    tm = _row_tile(n, ROW_TILE)
    return pl.pallas_call(
        _in_proj_kernel,
        grid=(n // tm, N_GROUPS),
        in_specs=[
            pl.BlockSpec((tm, D_MODEL), lambda i, g: (i, 0)),
            pl.BlockSpec((1, D_MODEL), lambda i, g: (0, 0)),
            pl.BlockSpec((D_MODEL, D_MODEL), lambda i, g: (0, g)),
            pl.BlockSpec((SMALL_ROWS, D_MODEL), lambda i, g: (0, 0)),
        ],
        out_specs=[
            pl.BlockSpec((None, tm, D_MODEL), lambda i, g: (g, i, 0)),
            pl.BlockSpec((SMALL_ROWS, tm), lambda i, g: (0, i)),
        ],
        out_shape=[
            jax.ShapeDtypeStruct((N_GROUPS, n, D_MODEL), F32),
            jax.ShapeDtypeStruct((SMALL_ROWS, n), F32),
        ],
        scratch_shapes=[pltpu.VMEM((tm, D_MODEL), BF16)],
        compiler_params=_params(("arbitrary", "arbitrary"), 48),
        name="in_proj",
    )(x, gain, w_big, w_small_t)


def _logf_kernel(st_ref, bf_ref, lf_ref, c_ref, *, seg):
    lf = _log_sigmoid(st_ref[16:24, :] + bf_ref[...])
    lf_ref[...] = lf
    pos = lax.broadcasted_iota(I32, lf.shape, 1) % seg
    y = lf
    k = 1
    while k < seg:
        y = y + jnp.where(pos >= k, pltpu.roll(y, k, axis=1), 0.0)
        k *= 2
    c_ref[...] = y


def logf_scan(small_t, b_f, seg, block):
    n = small_t.shape[1]
    return pl.pallas_call(
        functools.partial(_logf_kernel, seg=seg),
        grid=(n // block,),
        in_specs=[
            pl.BlockSpec((SMALL_ROWS, block), lambda i: (0, i)),
            pl.BlockSpec((N_HEADS, 1), lambda i: (0, 0)),
        ],
        out_specs=[
            pl.BlockSpec((N_HEADS, block), lambda i: (0, i)),
            pl.BlockSpec((N_HEADS, block), lambda i: (0, i)),
        ],
        out_shape=[jax.ShapeDtypeStruct((N_HEADS, n), F32)] * 2,
        compiler_params=_params(("arbitrary",),16),
        name="logf_scan",
    )(small_t, b_f)


def _unit_lower_inverse(m, c, ii, jj, eye):
    def same_block(s):
        return (ii // s) == (jj // s)

    hs = range(len(m))
    d = [jnp.where(same_block(8), m[h], 0.0) for h in hs]
    d2 = [_bdot(d[h], d[h]) for h in hs]
    d4 = [_bdot(d2[h], d2[h]) for h in hs]
    x = [eye - d[h] for h in hs]
    x = [x[h] + _bdot(x[h], d2[h]) for h in hs]
    x = [x[h] + _bdot(x[h], d4[h]) for h in hs]
    s = 8
    while s < c:
        sel = same_block(2 * s) & jnp.logical_not(same_block(s))
        t = [_bdot(x[h], jnp.where(sel, m[h], 0.0)) for h in hs]
        x = [x[h] - _bdot(t[h], x[h]) for h in hs]
        s *= 2
    return x


def _gdn_kernel(q_ref, k_ref, v_ref, z_ref, sm_ref, cb_ref, s0_ref, cw_ref, al_ref, dt_ref, no_ref,
                o_ref, s_ref, full_ref, *, c, nck):
    n = pl.program_id(1)
    rows = nck * c

    @pl.when(n == 0)
    def _():
        full_ref[0:8, :] = cb_ref[...]
        s_ref[...] = s0_ref[...]

    full_ref[8:8 + rows, 0:D_MODEL] = q_ref[...]
    full_ref[8:8 + rows, D_MODEL:2 * D_MODEL] = k_ref[...]
    full_ref[8:8 + rows, 2 * D_MODEL:3 * D_MODEL] = v_ref[...]

    conv = cw_ref[0:1, :] * full_ref[5:5 + rows, :]
    for i in range(1, CONV_W):
        conv = conv + cw_ref[i:i + 1, :] * full_ref[5 + i:5 + i + rows, :]
    u_all = _silu(conv)
    tail = full_ref[rows:rows + 8, :]
    full_ref[0:8, :] = tail

    ii = lax.broadcasted_iota(I32, (c, c), 0)
    jj = lax.broadcasted_iota(I32, (c, c), 1)
    incl = ii >= jj
    strict = ii > jj
    eye = (ii == jj).astype(F32)
    lower_b = incl.astype(BF16)
    upper_b = (ii <= jj).astype(BF16)
    eye_b = eye.astype(BF16)

    g_row, g_col, beta_col = [], [], []
    for j in range(nck):
        sm = sm_ref[j]
        beta_all = _sigmoid(sm[0:8, :])
        g_all = -jnp.exp(al_ref[...]) * _softplus(sm[8:16, :] + dt_ref[...])
        gr_, gc_, bc_ = (jnp.zeros((N_HEADS, c), F32), jnp.zeros((c, N_HEADS), F32),
                         jnp.zeros((c, N_HEADS), F32))
        for gp, bp in zip(_split3(g_all), _split3(beta_all)):
            gr_ = gr_ + jnp.dot(gp, upper_b, preferred_element_type=F32)
            gc_ = gc_ + lax.dot_general(lower_b, gp, (((1,), (1,)), ((), ())), preferred_element_type=F32)
            bc_ = bc_ + lax.dot_general(eye_b, bp, (((1,), (1,)), ((), ())), preferred_element_type=F32)
        g_row.append(gr_)
        g_col.append(gc_)
        beta_col.append(bc_)

    items = [(j, h) for j in range(nck) for h in range(N_HEADS)]
    it = range(len(items))
    sls = [slice(h * HEAD_DIM, (h + 1) * HEAD_DIM) for h in range(N_HEADS)]
    qh, kh, vh = [], [], []
    for j, h in items:
        rs = slice(j * c, (j + 1) * c)
        q_ = u_all[rs, h * HEAD_DIM:(h + 1) * HEAD_DIM]
        k_ = u_all[rs, D_MODEL + h * HEAD_DIM:D_MODEL + (h + 1) * HEAD_DIM]
        qh.append(q_ * lax.rsqrt(jnp.sum(q_ * q_, axis=-1, keepdims=True) + EPS) * (HEAD_DIM ** -0.5))
        kh.append(k_ * lax.rsqrt(jnp.sum(k_ * k_, axis=-1, keepdims=True) + EPS))
        vh.append(u_all[rs, 2 * D_MODEL + h * HEAD_DIM:2 * D_MODEL + (h + 1) * HEAD_DIM])
    gc = [g_col[j][:, h:h + 1] for j, h in items]
    gr = [g_row[j][h:h + 1, :] for j, h in items]
    bc = [beta_col[j][:, h:h + 1] for j, h in items]
    decay = [jnp.where(incl, jnp.exp(jnp.where(incl, gc[i] - gr[i], 0.0)), 0.0) for i in it]
    kb = [kh[i] * bc[i] for i in it]
    kk = [_bdot_nt(kb[i], kh[i]) for i in it]
    qk = [_bdot_nt(qh[i], kh[i]) * decay[i] for i in it]
    m = [jnp.where(strict, kk[i] * decay[i], 0.0) for i in it]
    eg = [jnp.exp(gc[i]) for i in it]
    rhs = [jnp.concatenate([vh[i] * bc[i], kb[i] * eg[i]], axis=1) for i in it]
    inv = _unit_lower_inverse(m, c, ii, jj, eye)
    sol = [_bdot(inv[i], rhs[i]) for i in it]
    g_last = [gr[i][:, c - 1:c] for i in it]
    q_dec = [qh[i] * eg[i] for i in it]
    k_dec = [kh[i] * jnp.exp(g_last[i] - gc[i]) for i in it]

    s_cur = [s_ref[h] for h in range(N_HEADS)]
    for j in range(nck):
        ids = [j * N_HEADS + h for h in range(N_HEADS)]
        w_s = [_bdot(sol[i][:, HEAD_DIM:], s_cur[h]) for h, i in enumerate(ids)]
        q_s = [_bdot(q_dec[i], s_cur[h]) for h, i in enumerate(ids)]
        uu = [sol[i][:, :HEAD_DIM] - w_s[h] for h, i in enumerate(ids)]
        o = [q_s[h] + _bdot(qk[i], uu[h]) for h, i in enumerate(ids)]
        s_upd = [_bdot_tn(k_dec[i], uu[h]) for h, i in enumerate(ids)]
        s_cur = [s_cur[h] * jnp.exp(g_last[i]) + s_upd[h] for h, i in enumerate(ids)]
        for h in range(N_HEADS):
            o_ref[j * c:(j + 1) * c, sls[h]] = (_rms(o[h], no_ref[...])
                                                * _silu(z_ref[j * c:(j + 1) * c, sls[h]]))
    for h in range(N_HEADS):
        s_ref[h] = s_cur[h]


def gdn(big, small_c, conv_buf, s0, conv_w, a_log, dt_bias, norm_o, batch, length, c, tok0):
    n_tok = batch * length
    nck = GDN_CHUNKS_PER_STEP if (length // c) % GDN_CHUNKS_PER_STEP == 0 else 1
    rows = nck * c
    nc = length // rows
    blk0 = tok0 // rows

    def grp(g):
        return pl.BlockSpec((None, rows, D_MODEL), lambda b, n, g=g: (g, blk0 + b * nc + n, 0))

    def full(shape):
        return pl.BlockSpec(shape, lambda b, n: (0,) * len(shape))

    return pl.pallas_call(
        functools.partial(_gdn_kernel, c=c, nck=nck),
        grid=(batch, nc),
        in_specs=[
            grp(G_QA), grp(G_KA), grp(G_VA), grp(G_ZA),
            pl.BlockSpec((nck, SMALL_ROWS, c), lambda b, n: (b * nc + n, 0, 0)),
            pl.BlockSpec((None, 8, CONV_CH), lambda b, n: (b, 0, 0)),
            pl.BlockSpec((None, N_HEADS, HEAD_DIM, HEAD_DIM), lambda b, n: (b, 0, 0, 0)),
            full((CONV_W, CONV_CH)), full((N_HEADS, 1)), full((N_HEADS, 1)), full((1, HEAD_DIM)),
        ],
        out_specs=[
            pl.BlockSpec((rows, D_MODEL), lambda b, n: (b * nc + n, 0)),
            pl.BlockSpec((None, N_HEADS, HEAD_DIM, HEAD_DIM), lambda b, n: (b, 0, 0, 0)),
        ],
        out_shape=[
            jax.ShapeDtypeStruct((n_tok, D_MODEL), F32),
            jax.ShapeDtypeStruct((batch, N_HEADS, HEAD_DIM, HEAD_DIM), F32),
        ],
        scratch_shapes=[pltpu.VMEM((rows + 8, CONV_CH), F32)],
        compiler_params=_params(("arbitrary", "arbitrary"),32),
        name="gdn",
    )(big, big, big, big, small_c, conv_buf, s0, conv_w, a_log, dt_bias, norm_o)


def _fox_prompt_kernel(q_ref, k_ref, v_ref, c_ref, o_ref, *, tq, tk):
    qi = pl.program_id(2)
    q0 = pl.multiple_of(qi * tq, tq)
    c_base = jnp.max(c_ref[:, pl.ds(q0, 128)], axis=1, keepdims=True)
    n_parts = 2 if tq % 16 == 0 else 1
    rp = tq // n_parts
    parts = [slice(i * rp, (i + 1) * rp) for i in range(n_parts)]
    q_parts = [(q_ref[sl, :] * (HEAD_DIM ** -0.5 * LOG2_E)).astype(BF16) for sl in parts]
    rows = q0 + lax.broadcasted_iota(I32, (rp, tk), 0)
    cols = lax.broadcasted_iota(I32, (rp, tk), 1)

    def block(ki, carry, masked):
        k0 = pl.multiple_of(ki * tk, tk)
        kb = k_ref[pl.ds(k0, tk), :].astype(BF16)
        vb = v_ref[pl.ds(k0, tk), :].astype(BF16)
        bias = (c_base - c_ref[:, pl.ds(k0, tk)]) * LOG2_E
        ss = [_bdot_nt(qp, kb) + bias for qp in q_parts]
        if masked:
            ss = [jnp.where(cols + k0 <= rows + i * rp, s, NEG_BIG) for i, s in enumerate(ss)]
        ms, ls, ps = [], [], []
        for (m, l, _), s in zip(carry, ss):
            m_new = jnp.maximum(m, jnp.max(s, axis=1, keepdims=True))
            alpha = jnp.exp2(m - m_new)
            p = jnp.exp2(s - m_new)
            ms.append((m_new, alpha))
            ls.append(l * alpha + jnp.sum(p, axis=1, keepdims=True))
            ps.append(p)
        pvs = [_bdot(p, vb) for p in ps]
        return tuple((m_new, l, acc * alpha + pv)
                     for (m_new, alpha), l, (_, _, acc), pv in zip(ms, ls, carry, pvs))

    init = tuple((jnp.full((rp, 1), NEG_BIG, F32), jnp.zeros((rp, 1), F32),
                  jnp.zeros((rp, HEAD_DIM), F32)) for _ in parts)
    carry = lax.fori_loop(0, qi, lambda ki, c: block(ki, c, False), init)
    carry = block(qi, carry, True)
    o_ref[...] = jnp.concatenate([acc / l for (_, l, acc) in carry], axis=0)


def fox_prompt(big, c_t, batch, seq):
    tq = min(512, seq)
    tk = tq
    nq = seq // tq

    def grp(g):
        return pl.BlockSpec((None, seq, HEAD_DIM), lambda b, h, qi, g=g: (g, b, h))

    return pl.pallas_call(
        functools.partial(_fox_prompt_kernel, tq=tq, tk=tk),
        grid=(batch, N_HEADS, nq),
        in_specs=[
            pl.BlockSpec((None, tq, HEAD_DIM), lambda b, h, qi: (G_QB, b * nq + qi, h)),
            grp(G_KB), grp(G_VB),
            pl.BlockSpec((None, None, 1, seq), lambda b, h, qi: (b, h, 0, 0)),
        ],
        out_specs=pl.BlockSpec((tq, HEAD_DIM), lambda b, h, qi: (b * nq + qi, h)),
        out_shape=jax.ShapeDtypeStruct((batch * seq, D_MODEL), F32),
        compiler_params=_params(("arbitrary", "arbitrary", "arbitrary"), 32),
        name="fox_prompt",
    )(big, big, big, c_t)


def _fox_sample_kernel(pt_ref, *refs, n_new, pps):
    del pt_ref
    k_pages = refs[0:pps]
    v_pages = refs[pps:2 * pps]
    f_pages = refs[2 * pps:3 * pps]
    q_ref, kn_ref, vn_ref, cn_ref = refs[3 * pps:3 * pps + 4]
    o_ref = refs[3 * pps + 4]
    qf_ref, m_ref, l_ref, acc_ref, carry_ref = refs[3 * pps + 5:]
    step = pl.program_id(1)
    n_rows = N_HEADS * n_new
    page_cols = PAGE_SIZE * N_HEADS

    def by_head(x):
        return jnp.concatenate([x[:, h * HEAD_DIM:(h + 1) * HEAD_DIM] for h in range(N_HEADS)], axis=0)

    @pl.when(step == 0)
    def _():
        qf = by_head(q_ref[...] * (HEAD_DIM ** -0.5)).astype(BF16)
        qf_ref[...] = qf
        s = _bdot_nt(qf, by_head(kn_ref[...])) - cn_ref[...]
        r_i = lax.broadcasted_iota(I32, (n_rows, n_rows), 0)
        c_i = lax.broadcasted_iota(I32, (n_rows, n_rows), 1)
        ok = (r_i // n_new == c_i // n_new) & (c_i % n_new <= r_i % n_new)
        s = jnp.where(ok, s, NEG_BIG)
        m = jnp.max(s, axis=1, keepdims=True)
        p = jnp.exp(s - m)
        m_ref[...] = m
        l_ref[...] = jnp.sum(p, axis=1, keepdims=True)
        acc_ref[...] = _bdot(p, by_head(vn_ref[...]))
        carry_ref[...] = jnp.zeros_like(carry_ref)

    r_h = lax.broadcasted_iota(I32, (n_rows, page_cols), 0) // n_new
    c_h = lax.broadcasted_iota(I32, (n_rows, page_cols), 1) % N_HEADS
    same_head = r_h == c_h
    qf = qf_ref[...]
    carry = carry_ref[...]

    bias = [None] * pps
    for g0 in range(pps - SCAN_PAGES, -1, -SCAN_PAGES):
        lf = jnp.concatenate([f_pages[g0 + i][...] for i in range(SCAN_PAGES)], axis=0)
        lane = lax.broadcasted_iota(I32, lf.shape, 1)
        row = lax.broadcasted_iota(I32, lf.shape, 0)
        y = lf
        tot = lf
        k = N_HEADS
        while k < page_cols:
            y = y + jnp.where(lane + k < page_cols, pltpu.roll(y, page_cols - k, axis=1), 0.0)
            tot = tot + pltpu.roll(tot, page_cols - k, axis=1)
            k *= 2
        later = tot
        k = 1
        while k < SCAN_PAGES:
            later = later + jnp.where(row + k < SCAN_PAGES, pltpu.roll(later, SCAN_PAGES - k, axis=0), 0.0)
            k *= 2
        r = (y - lf) + (later - tot) + carry
        carry = carry + later[0:1, :]
        for i in range(SCAN_PAGES):
            bias[g0 + i] = r[i:i + 1, :]
    carry_ref[...] = carry

    def group_scores(g):
        cols = []
        for p_i in range(g * SOFTMAX_PAGES, (g + 1) * SOFTMAX_PAGES):
            k2 = k_pages[p_i][...].reshape(page_cols, HEAD_DIM)
            cols.append(jnp.where(same_head, _bdot_nt(qf, k2) + bias[p_i], NEG_BIG))
        return jnp.concatenate(cols, axis=1)

    m, l, acc = m_ref[...], l_ref[...], acc_ref[...]
    n_groups = pps // SOFTMAX_PAGES
    s_next = group_scores(0)
    for g in range(n_groups):
        s = s_next
        if g + 1 < n_groups:
            s_next = group_scores(g + 1)
        m_new = jnp.maximum(m, jnp.max(s, axis=1, keepdims=True))
        alpha = jnp.exp(m - m_new)
        p = jnp.exp(s - m_new)
        l = l * alpha + jnp.sum(p, axis=1, keepdims=True)
        pv = jnp.zeros((n_rows, HEAD_DIM), F32)
        for i in range(SOFTMAX_PAGES):
            v2 = v_pages[g * SOFTMAX_PAGES + i][...].reshape(page_cols, HEAD_DIM)
            pv = pv + _bdot(p[:, i * page_cols:(i + 1) * page_cols], v2)
        acc = acc * alpha + pv
        m = m_new
    m_ref[...] = m
    l_ref[...] = l
    acc_ref[...] = acc

    @pl.when(step == pl.num_programs(1) - 1)
    def _():
        out = acc_ref[...] / l_ref[...]
        o_ref[...] = jnp.concatenate([out[h * n_new:(h + 1) * n_new, :] for h in range(N_HEADS)], axis=1)


def fox_sample(big, cn, cache_k, cache_v, cache_f, page_table, n_seq, n_new, tok0):
    n_pages = page_table.shape[1]
    pps = min(PAGES_PER_STEP, n_pages)
    assert n_pages % pps == 0 and pps % SCAN_PAGES == 0, (n_pages, pps)
    n_steps = n_pages // pps
    n_rows = N_HEADS * n_new
    page_cols = PAGE_SIZE * N_HEADS
    blk0 = tok0 // n_new

    def page_spec(shape, p_i):
        return pl.BlockSpec(shape, lambda b, j, pt, p_i=p_i: (
            (pt[b * n_pages + (n_steps - 1 - j) * pps + p_i],) + (0,) * (len(shape) - 1)))

    def grp(g):
        return pl.BlockSpec((None, n_new, D_MODEL), lambda b, j, pt, g=g: (g, blk0 + b, 0))

    kv_shape = (None, PAGE_SIZE, N_HEADS, HEAD_DIM)
    in_specs = ([page_spec(kv_shape, p_i) for p_i in range(pps)]
                + [page_spec(kv_shape, p_i) for p_i in range(pps)]
                + [page_spec((None, 1, page_cols), p_i) for p_i in range(pps)]
                + [grp(G_QB), grp(G_KB), grp(G_VB),
                   pl.BlockSpec((None, 1, n_rows), lambda b, j, pt: (b, 0, 0))])
    grid_spec = pltpu.PrefetchScalarGridSpec(
        num_scalar_prefetch=1,
        grid=(n_seq, n_steps),
        in_specs=in_specs,
        out_specs=pl.BlockSpec((n_new, D_MODEL), lambda b, j, pt: (b, 0)),
        scratch_shapes=[
            pltpu.VMEM((n_rows, HEAD_DIM), BF16),
            pltpu.VMEM((n_rows, 1), F32),
            pltpu.VMEM((n_rows, 1), F32),
            pltpu.VMEM((n_rows, HEAD_DIM), F32),
            pltpu.VMEM((1, page_cols), F32),
        ],
    )
    args = [cache_k] * pps + [cache_v] * pps + [cache_f] * pps + [big, big, big, cn]
    return pl.pallas_call(
        functools.partial(_fox_sample_kernel, n_new=n_new, pps=pps),
        grid_spec=grid_spec,
        out_shape=jax.ShapeDtypeStruct((n_seq * n_new, D_MODEL), F32),
        compiler_params=_params(("arbitrary", "arbitrary"), 48),
        name="fox_sample",
    )(page_table.reshape(-1), *args)


def _merge_kernel(x_ref, oap_ref, oas_ref, obp_ref, obs_ref, ga_ref, gb_ref, woa_ref, wob_ref,
                  wout_ref, nf_ref, wr_ref, br_ref, x1_ref, h2_ref, idx_ref, wts_ref, rank_ref,
                  cnt_ref, *, prompt_tiles):
    is_prompt = pl.program_id(0) < prompt_tiles
    oa = jnp.where(is_prompt, oap_ref[...], oas_ref[...])
    ob = jnp.where(is_prompt, obp_ref[...], obs_ref[...])
    ya = _bdot(oa, woa_ref[...])
    yb = _bdot(ob, wob_ref[...])
    mixed = _sigmoid(ga_ref[...]) * ya + _sigmoid(gb_ref[...]) * yb
    x1 = x_ref[...] + _bdot(mixed, wout_ref[...])
    x1_ref[...] = x1
    h2 = _rms(x1, nf_ref[...])
    h2_ref[...] = h2

    logits = jnp.dot(h2, wr_ref[...], preferred_element_type=F32,
                     precision=lax.Precision.HIGHEST) + br_ref[...]
    tm = logits.shape[0]
    lane = lax.broadcasted_iota(I32, logits.shape, 1).astype(F32)
    vals, idxs = [], []
    work = logits
    for _ in range(TOP_K):
        v = jnp.max(work, axis=1, keepdims=True)
        i = jnp.min(jnp.where(work == v, lane, float(ROUTER_LANES)), axis=1, keepdims=True)
        vals.append(v)
        idxs.append(i)
        work = jnp.where(lane == i, -jnp.inf, work)
    es = [jnp.exp(v - vals[0]) for v in vals]
    denom = es[0] + es[1] + es[2] + es[3]

    onehot = jnp.zeros(logits.shape, F32)
    for i in idxs:
        onehot = onehot + (lane == i).astype(F32)
    ti = lax.broadcasted_iota(I32, (tm, tm), 0)
    tj = lax.broadcasted_iota(I32, (tm, tm), 1)
    before = jnp.dot((tj < ti).astype(BF16), onehot.astype(BF16), preferred_element_type=F32)
    cnt_ref[...] = jnp.sum(onehot, axis=0, keepdims=True).astype(I32)

    l4 = lax.broadcasted_iota(I32, (tm, TOP_K), 1)
    idx_out = jnp.zeros((tm, TOP_K), I32)
    wts_out = jnp.zeros((tm, TOP_K), F32)
    rank_out = jnp.zeros((tm, TOP_K), I32)
    for k in range(TOP_K):
        rk = jnp.sum(jnp.where(lane == idxs[k], before, 0.0), axis=1, keepdims=True)
        idx_out = jnp.where(l4 == k, idxs[k].astype(I32), idx_out)
        wts_out = jnp.where(l4 == k, es[k] / denom, wts_out)
        rank_out = jnp.where(l4 == k, rk.astype(I32), rank_out)
    idx_ref[...] = idx_out
    wts_ref[...] = wts_out
    rank_ref[...] = rank_out


def merge(x, oa_p, oa_s, ob_p, ob_s, big, w_oa, w_ob, w_out, norm_ffn, w_router, b_router):
    n = x.shape[0]
    tm = _row_tile(oa_s.shape[0], _row_tile(oa_p.shape[0], MERGE_TILE))
    nt = n // tm
    ntp = oa_p.shape[0] // tm
    nts = oa_s.shape[0] // tm
    row = pl.BlockSpec((tm, D_MODEL), lambda i: (i, 0))
    row_p = pl.BlockSpec((tm, D_MODEL), lambda i: (jnp.minimum(i, ntp - 1), 0))
    row_s = pl.BlockSpec((tm, D_MODEL), lambda i: (jnp.clip(i - ntp, 0, nts - 1), 0))
    wfull = pl.BlockSpec((D_MODEL, D_MODEL), lambda i: (0, 0))
    k4 = pl.BlockSpec((tm, TOP_K), lambda i: (i, 0))
    return pl.pallas_call(
        functools.partial(_merge_kernel, prompt_tiles=ntp),
        grid=(nt,),
        in_specs=[
            row, row_p, row_s, row_p, row_s,
            pl.BlockSpec((None, tm, D_MODEL), lambda i: (G_GA, i, 0)),
            pl.BlockSpec((None, tm, D_MODEL), lambda i: (G_GB, i, 0)),
            wfull, wfull, wfull,
            pl.BlockSpec((1, D_MODEL), lambda i: (0, 0)),
            pl.BlockSpec((D_MODEL, ROUTER_LANES), lambda i: (0, 0)),
            pl.BlockSpec((1, ROUTER_LANES), lambda i: (0, 0)),
        ],
        out_specs=[row, row, k4, k4, k4, pl.BlockSpec((None, 1, ROUTER_LANES), lambda i: (i, 0, 0))],
        out_shape=[
            jax.ShapeDtypeStruct((n, D_MODEL), F32),
            jax.ShapeDtypeStruct((n, D_MODEL), F32),
            jax.ShapeDtypeStruct((n, TOP_K), I32),
            jax.ShapeDtypeStruct((n, TOP_K), F32),
            jax.ShapeDtypeStruct((n, TOP_K), I32),
            jax.ShapeDtypeStruct((nt, 1, ROUTER_LANES), I32),
        ],
        compiler_params=_params(("arbitrary",),56),
        name="merge",
    )(x, oa_p, oa_s, ob_p, ob_s, big, big, w_oa, w_ob, w_out, norm_ffn, w_router, b_router)


def _row_copy(src_ref, src_row, dst_ref, dst_row, sem):
    return pltpu.make_async_copy(src_ref.at[pl.ds(src_row, 1), :], dst_ref.at[pl.ds(dst_row, 1), :], sem)


def _dispatch_kernel(h_ref, dest_ref, xs_in_ref, xs_ref, sem):
    del xs_in_ref
    n_copies = dest_ref.shape[0]

    tm = h_ref.shape[0]
    assert n_copies == tm * TOP_K

    def start(t, carry):
        for k in range(TOP_K):
            _row_copy(h_ref, t, xs_ref, dest_ref[t * TOP_K + k], sem).start()
        return carry

    lax.fori_loop(0, tm, start, 0, unroll=DMA_UNROLL)
    for _ in range(TOP_K):
        pltpu.make_async_copy(h_ref, xs_ref.at[pl.ds(0, tm), :], sem).wait()


def dispatch(h2, dest_flat, n_rows):
    n = h2.shape[0]
    tm = _row_tile(n, GATHER_TILE)
    xs_init = jnp.zeros((n_rows, D_MODEL), F32)
    return pl.pallas_call(
        _dispatch_kernel,
        grid=(n // tm,),
        in_specs=[
            pl.BlockSpec((tm, D_MODEL), lambda i: (i, 0)),
            pl.BlockSpec((tm * TOP_K,), lambda i: (i,), memory_space=pltpu.SMEM),
            pl.BlockSpec(memory_space=pl.ANY),
        ],
        out_specs=pl.BlockSpec(memory_space=pl.ANY),
        out_shape=jax.ShapeDtypeStruct((n_rows, D_MODEL), F32),
        scratch_shapes=[pltpu.SemaphoreType.DMA(())],
        input_output_aliases={2: 0},
        compiler_params=_params(("arbitrary",), 16),
        name="dispatch",
    )(h2, dest_flat, xs_init)


def _moe_kernel(te_ref, nt_ref, x_ref, wg_ref, bg_ref, wu_ref, bu_ref, wd_ref, bd_ref, y_ref,
                wgb_ref, wub_ref, wdb_ref):
    g = pl.program_id(0)
    active = g < nt_ref[0]
    new_expert = jnp.logical_or(g == 0, te_ref[g] != te_ref[jnp.maximum(g - 1, 0)])

    @pl.when(jnp.logical_and(active, new_expert))
    def _():
        wgb_ref[...] = wg_ref[...].astype(BF16)
        wub_ref[...] = wu_ref[...].astype(BF16)
        wdb_ref[...] = wd_ref[...].astype(BF16)

    @pl.when(active)
    def _():
        x = x_ref[...].astype(BF16)
        gt = jnp.minimum(jnp.dot(x, wgb_ref[...], preferred_element_type=F32) + bg_ref[...], SWIGLU_LIMIT)
        up = jnp.clip(jnp.dot(x, wub_ref[...], preferred_element_type=F32) + bu_ref[...],
                      -SWIGLU_LIMIT, SWIGLU_LIMIT)
        act = (up + 1.0) * gt * _sigmoid(SWIGLU_ALPHA * gt)
        y_ref[...] = _bdot(act, wdb_ref[...]) + bd_ref[...]

    @pl.when(g >= nt_ref[0])
    def _():
        y_ref[...] = jnp.zeros_like(y_ref)


def moe(xs, tile_expert, n_tiles, w_gate, b_gate, w_up, b_up, w_down, b_down):
    n_rows = xs.shape[0]
    tm = MOE_TILE
    wspec = pl.BlockSpec((None, D_MODEL, D_MODEL), lambda g, te, nt: (te[g], 0, 0))
    bspec = pl.BlockSpec((None, 1, D_MODEL), lambda g, te, nt: (te[g], 0, 0))
    row = pl.BlockSpec((tm, D_MODEL), lambda g, te, nt: (g, 0))
    grid_spec = pltpu.PrefetchScalarGridSpec(
        num_scalar_prefetch=2,
        grid=(n_rows // tm,),
        in_specs=[row, wspec, bspec, wspec, bspec, wspec, bspec],
        out_specs=row,
        scratch_shapes=[pltpu.VMEM((D_MODEL, D_MODEL), BF16)] * 3,
    )
    return pl.pallas_call(
        _moe_kernel,
        grid_spec=grid_spec,
        out_shape=jax.ShapeDtypeStruct((n_rows, D_MODEL), F32),
        compiler_params=_params(("arbitrary",), 56),
        name="moe",
    )(tile_expert, n_tiles, xs, w_gate, b_gate, w_up, b_up, w_down, b_down)


def _combine_kernel(x1_ref, wts_ref, pe_ref, dest_ref, ys_ref, np_ref, wpg_ref, wpp_ref, nfin_ref,
                    y_ref, buf_ref, sem):
    tm = x1_ref.shape[0]
    n_copies = tm * TOP_K

    def start(t, carry):
        for k in range(TOP_K):
            _row_copy(ys_ref, dest_ref[t * TOP_K + k], buf_ref.at[k], t, sem).start()
        return carry

    lax.fori_loop(0, tm, start, 0, unroll=DMA_UNROLL)
    pe_proj = _bdot(pe_ref[...], wpp_ref[...])
    for k in range(TOP_K):
        pltpu.make_async_copy(ys_ref.at[pl.ds(0, tm), :], buf_ref.at[k], sem).wait()

    w = wts_ref[...]
    x2 = x1_ref[...]
    for k in range(TOP_K):
        x2 = x2 + w[:, k:k + 1] * buf_ref[k]
    gate = _sigmoid(_bdot(_rms(x2, np_ref[...]), wpg_ref[...]))
    x3 = x2 + pe_proj * gate
    y_ref[...] = _rms(x3, nfin_ref[...])


def combine(x1, wts, pe, dest_flat, ys, norm_ple, w_ple_gate, w_ple_proj, norm_final):
    n = x1.shape[0]
    tm = _row_tile(n, GATHER_TILE)
    ple = pe.shape[1]
    row = pl.BlockSpec((tm, D_MODEL), lambda i: (i, 0))
    vec = pl.BlockSpec((1, D_MODEL), lambda i: (0, 0))
    return pl.pallas_call(
        _combine_kernel,
        grid=(n // tm,),
        in_specs=[
            row,
            pl.BlockSpec((tm, TOP_K), lambda i: (i, 0)),
            pl.BlockSpec((tm, ple), lambda i: (i, 0)),
            pl.BlockSpec((tm * TOP_K,), lambda i: (i,), memory_space=pltpu.SMEM),
            pl.BlockSpec(memory_space=pl.ANY),
            vec,
            pl.BlockSpec((D_MODEL, D_MODEL), lambda i: (0, 0)),
            pl.BlockSpec((ple, D_MODEL), lambda i: (0, 0)),
            vec,
        ],
        out_specs=row,
        out_shape=jax.ShapeDtypeStruct((n, D_MODEL), F32),
        scratch_shapes=[pltpu.VMEM((TOP_K, tm, D_MODEL), F32), pltpu.SemaphoreType.DMA(())],
        compiler_params=_params(("arbitrary",), 32),
        name="combine",
    )(x1, wts, pe, dest_flat, ys, norm_ple, w_ple_gate, w_ple_proj, norm_final)


def _routing_tables(idx, rank, counts):
    n = idx.shape[0]
    tile = n // counts.shape[0]
    cnt = counts[:, 0, :N_EXPERTS]
    base = jnp.cumsum(cnt, axis=0) - cnt
    total = jnp.sum(cnt, axis=0)
    tiles_per_e = (total + MOE_TILE - 1) // MOE_TILE
    tile_end = jnp.cumsum(tiles_per_e)
    start_row = (tile_end - tiles_per_e) * MOE_TILE
    offs = (start_row[None, :] + base).astype(I32)
    offs_tok = jnp.broadcast_to(offs[:, None, None, :], (n // tile, tile, 1, N_EXPERTS))
    onehot = idx.reshape(n // tile, tile, TOP_K, 1) == jnp.arange(N_EXPERTS, dtype=I32)
    dest = jnp.sum(jnp.where(onehot, offs_tok, 0), axis=-1).reshape(n, TOP_K) + rank
    n_rows = (pl.cdiv(n * TOP_K, MOE_TILE) + N_EXPERTS) * MOE_TILE
    g = jnp.arange(n_rows // MOE_TILE, dtype=I32)
    n_tiles = tile_end[-1].astype(I32)
    g_eff = jnp.minimum(g, n_tiles - 1)
    tile_expert = jnp.sum((g_eff[:, None] >= tile_end[None, :]).astype(I32), axis=1)
    return dest.reshape(-1).astype(I32), tile_expert, n_tiles.reshape(1), n_rows


def kernel(x_prompt, x_sample, p_prompt, p_sample, cache_k, cache_v, cache_logf, state_S, state_conv,
           page_table, norm_mix, w_in, conv_w, a_log, dt_bias, norm_o_a, w_o_a, b_f, w_o_b, w_out,
           norm_ffn, w_router, b_router, w_gate, b_gate, w_up, b_up, w_down, b_down, norm_ple,
           w_ple_gate, w_ple_proj, norm_final):
    batch, seq, _ = x_prompt.shape
    n_seq, n_new, _ = x_sample.shape
    n_p = batch * seq
    n_s = n_seq * n_new
    n_pool = cache_k.shape[1]

    w = w_in[0]
    w_big = jnp.concatenate([w[:, 0:4096], w[:, 4112:7184], w[:, 7192:9240]], axis=1).astype(BF16)
    w_small = jnp.concatenate([w[:, 4096:4112], w[:, 7184:7192]], axis=1)
    w_small_t = jnp.pad(w_small.T, ((0, SMALL_ROWS - 24), (0, 0))).astype(BF16)
    gain_mix = norm_mix[0].reshape(1, D_MODEL)
    col = lambda v: v[0].reshape(N_HEADS, 1)
    w_oa_b, w_ob_b, w_out_b = (t[0].astype(BF16) for t in (w_o_a, w_o_b, w_out))
    wg_b, wu_b, wd_b = w_gate[0], w_up[0], w_down[0]
    bg, bu, bd = (t[0].reshape(N_EXPERTS, 1, D_MODEL) for t in (b_gate, b_up, b_down))
    w_pg_b = w_ple_gate[0].astype(BF16)
    w_pp_b = w_ple_proj[0].astype(BF16)

    w_router_p = jnp.pad(w_router[0], ((0, 0), (0, ROUTER_LANES - N_EXPERTS)))
    b_router_p = jnp.pad(b_router[0], (0, ROUTER_LANES - N_EXPERTS),
                         constant_values=-jnp.inf).reshape(1, ROUTER_LANES)

    cat = lambda a, b: jnp.concatenate([a, b], axis=0)
    x_all = cat(x_prompt.reshape(n_p, D_MODEL), x_sample.reshape(n_s, D_MODEL))
    pe_all = cat(p_prompt[0].reshape(n_p, -1), p_sample[0].reshape(n_s, -1))

    big, small_t = in_proj(x_all, gain_mix, w_big, w_small_t)
    small_p, small_s = small_t[:, :n_p], small_t[:, n_p:]

    lf_p, c_p = logf_scan(small_p, col(b_f), seq, seq)
    lf_s, c_s = logf_scan(small_s, col(b_f), n_new, n_s)

    def gdn_group(small_g, conv_state, s0, b, length, tok0):
        c = min(DN_CHUNK, length)
        small_c = small_g.reshape(SMALL_ROWS, b * length // c, c).transpose(1, 0, 2)
        conv_buf = jnp.pad(conv_state, ((0, 0), (8 - (CONV_W - 1), 0), (0, 0)))
        return gdn(big, small_c, conv_buf, s0, conv_w[0], col(a_log), col(dt_bias),
                   norm_o_a[0].reshape(1, HEAD_DIM), b, length, c, tok0)

    oa_p, s_p = gdn_group(small_p, jnp.zeros((batch, CONV_W - 1, CONV_CH), F32),
                          jnp.zeros((batch, N_HEADS, HEAD_DIM, HEAD_DIM), F32), batch, seq, 0)
    oa_s, s_s = gdn_group(small_s, state_conv[0], state_S[0], n_seq, n_new, n_p)

    ob_p = fox_prompt(big, c_p.reshape(N_HEADS, batch, 1, seq).transpose(1, 0, 2, 3), batch, seq)
    cn = c_s.reshape(N_HEADS, n_seq, n_new).transpose(1, 0, 2).reshape(n_seq, 1, N_HEADS * n_new)
    cache_f = cache_logf[0].reshape(n_pool, 1, PAGE_SIZE * N_HEADS)
    ob_s = fox_sample(big, cn, cache_k[0], cache_v[0], cache_f, page_table, n_seq, n_new, n_p)

    x1, h2, idx, wts, rank, cnt = merge(x_all, oa_p, oa_s, ob_p, ob_s, big, w_oa_b, w_ob_b, w_out_b,
                                        norm_ffn[0].reshape(1, D_MODEL), w_router_p, b_router_p)
    dest_flat, tile_expert, n_tiles, n_rows = _routing_tables(idx, rank, cnt)
    xs_sorted = dispatch(h2, dest_flat, n_rows)
    ys = moe(xs_sorted, tile_expert, n_tiles, wg_b, bg, wu_b, bu, wd_b, bd)
    y_all = combine(x1, wts, pe_all, dest_flat, ys, norm_ple[0].reshape(1, D_MODEL), w_pg_b, w_pp_b,
                    norm_final.reshape(1, D_MODEL))

    def states(lf_t, s_new, b, length, tok0):
        rows = lambda g: big[g, tok0:tok0 + b * length]
        head = lambda g: rows(g).reshape(1, b, length, N_HEADS, HEAD_DIM)
        last = lambda g: rows(g).reshape(b, length, D_MODEL)[:, length - (CONV_W - 1):, :]
        pre = jnp.concatenate([last(G_QA), last(G_KA), last(G_VA)], axis=2)
        return (head(G_KB), head(G_VB), lf_t.T.reshape(1, b, length, N_HEADS), s_new[None], pre[None])

    st_p = states(lf_p, s_p, batch, seq, 0)
    st_s = states(lf_s, s_s, n_seq, n_new, n_p)
    return (y_all[:n_p].reshape(batch, seq, D_MODEL), y_all[n_p:].reshape(n_seq, n_new, D_MODEL)) + st_p + st_s
```
